```python
import jax, jax.numpy as jnp
from jax import lax
import numpy as np

D_MODEL = 1024
BATCH = 32
SEQ = 2048
DEPTH = 2

CHUNK = 64
N_MIXERS = 2
N_SB_LAYERS = (DEPTH + 1) // 2
N_TM_LAYERS = DEPTH // 2
SB_HEADS = 16
SB_HEAD_DIM = D_MODEL // SB_HEADS
SB_QBLOCK = 128
TM_CHUNK = 128
TM_GROUPS = 8
TM_GROUP_DIM = D_MODEL // TM_GROUPS
D_FF = -(-8 * D_MODEL // (3 * 256)) * 256
EPS = 1e-6

kernel_name = "hybrid_stickbreak_tokenmlp_swiglu"


def rms_norm(x, g):
    xf = x.astype(jnp.float32)
    y = xf * lax.rsqrt(jnp.mean(xf * xf, axis=-1, keepdims=True) + EPS)
    return (y * g.astype(jnp.float32)).astype(x.dtype)


def layer_norm(x, g):
    xf = x.astype(jnp.float32)
    mu = jnp.mean(xf, axis=-1, keepdims=True)
    xc = xf - mu
    y = xc * lax.rsqrt(jnp.mean(xc * xc, axis=-1, keepdims=True) + EPS)
    return (y * g.astype(jnp.float32)).astype(x.dtype)


def ada_modulation(c, w, b):
    m = jax.nn.silu(c) @ w + b
    shift, scale, gate = jnp.split(m, 3, axis=-1)
    return shift[:, None, :], scale[:, None, :], gate[:, None, :]


def stick_breaking_attention(q, k, v):
    S = q.shape[2]
    scale = SB_HEAD_DIM ** -0.5
    outs = []
    for qb in range(S // SB_QBLOCK):
        q0 = qb * SB_QBLOCK
        kend = q0 + SB_QBLOCK
        z = jnp.einsum('bhtd,bhsd->bhts', q[:, :, q0:kend], k[:, :, :kend]).astype(jnp.float32) * scale
        t_idx = q0 + jnp.arange(SB_QBLOCK)[:, None]
        s_idx = jnp.arange(kend)[None, :]
        before = s_idx < t_idx
        log_beta = jax.nn.log_sigmoid(z)
        log_1m = jnp.where(before, jax.nn.log_sigmoid(-z), 0.0)
        later = lax.cumsum(log_1m, axis=3, reverse=True) - log_1m
        a = jnp.where(before, jnp.exp(log_beta + later), 0.0)
        outs.append(jnp.einsum('bhts,bhsd->bhtd', a.astype(v.dtype), v[:, :, :kend]))
    return jnp.concatenate(outs, axis=2)


def stick_breaking_mixer(h, w_qkv, w_o):
    B, S, D = h.shape
    qkv = (h @ w_qkv).reshape(B, S, 3, SB_HEADS, SB_HEAD_DIM)
    q = qkv[:, :, 0].transpose(0, 2, 1, 3)
    k = qkv[:, :, 1].transpose(0, 2, 1, 3)
    v = qkv[:, :, 2].transpose(0, 2, 1, 3)
    o = stick_breaking_attention(q, k, v)
    return o.transpose(0, 2, 1, 3).reshape(B, S, D) @ w_o


def token_mixing_mixer(h, w_in, ln_g, w_s, b_s, w_out):
    B, S, D = h.shape
    uv = jax.nn.gelu(h @ w_in)
    u, v = jnp.split(uv, 2, axis=-1)
    v = layer_norm(v, ln_g)
    n = S // TM_CHUNK
    v = v.reshape(B, n, TM_CHUNK, TM_GROUPS, TM_GROUP_DIM)
    pos = jnp.arange(TM_CHUNK)
    chunk_causal = (pos[:, None] // CHUNK) >= (pos[None, :] // CHUNK)
    ws = jnp.where(chunk_causal[None], w_s, 0.0).astype(v.dtype)
    sv = jnp.einsum('gts,bnsgc->bntgc', ws, v) + b_s.T[None, None, :, :, None]
    y = u * sv.reshape(B, S, D)
    return y @ w_out


def swiglu(h, w_gate, w_up, w_down):
    return (jax.nn.silu(h @ w_gate) * (h @ w_up)) @ w_down


def setup_inputs(seed: int = 0) -> dict:
    key = jax.random.key(seed)
    ks = jax.random.split(key, 24)
    D = D_MODEL

    def nrm(k, shape, s):
        return jax.random.normal(k, shape, jnp.float32) * s

    return {
        "x": nrm(ks[0], (BATCH, SEQ, D), 1.0),
        "c": nrm(ks[1], (BATCH, D), 1.0),
        "mix_ada_w": nrm(ks[2], (DEPTH, D, 3 * D), 0.5 * D ** -0.5),
        "mix_ada_b": nrm(ks[3], (DEPTH, 3 * D), 0.02),
        "mix_pre_g": 1.0 + nrm(ks[4], (DEPTH, D), 0.02),
        "mix_post_g": 1.0 + nrm(ks[5], (DEPTH, D), 0.02),
        "sb_w_qkv": nrm(ks[6], (N_SB_LAYERS, D, 3 * D), D ** -0.5),
        "sb_w_o": nrm(ks[7], (N_SB_LAYERS, D, D), D ** -0.5),
        "tm_w_in": nrm(ks[8], (N_TM_LAYERS, D, 2 * D), D ** -0.5),
        "tm_ln_g": 1.0 + nrm(ks[9], (N_TM_LAYERS, D), 0.02),
        "tm_w_s": nrm(ks[10], (N_TM_LAYERS, TM_GROUPS, TM_CHUNK, TM_CHUNK), TM_CHUNK ** -0.5),
        "tm_b_s": 1.0 + nrm(ks[11], (N_TM_LAYERS, TM_GROUPS, TM_CHUNK), 0.02),
        "tm_w_out": nrm(ks[12], (N_TM_LAYERS, D, D), D ** -0.5),
        "ffn_ada_w": nrm(ks[13], (DEPTH, D, 3 * D), 0.5 * D ** -0.5),
        "ffn_ada_b": nrm(ks[14], (DEPTH, 3 * D), 0.02),
        "ffn_pre_g": 1.0 + nrm(ks[15], (DEPTH, D), 0.02),
        "ffn_post_g": 1.0 + nrm(ks[16], (DEPTH, D), 0.02),
        "ffn_w_gate": nrm(ks[17], (DEPTH, D, D_FF), D ** -0.5),
        "ffn_w_up": nrm(ks[18], (DEPTH, D, D_FF), D ** -0.5),
        "ffn_w_down": nrm(ks[19], (DEPTH, D_FF, D), D_FF ** -0.5),
    }


def reference(x, c, mix_ada_w, mix_ada_b, mix_pre_g, mix_post_g, sb_w_qkv, sb_w_o,
              tm_w_in, tm_ln_g, tm_w_s, tm_b_s, tm_w_out, ffn_ada_w, ffn_ada_b,
              ffn_pre_g, ffn_post_g, ffn_w_gate, ffn_w_up, ffn_w_down):
    for i in range(DEPTH):
        shift, scale, gate = ada_modulation(c, mix_ada_w[i], mix_ada_b[i])
        h = rms_norm(x, mix_pre_g[i]) * (1.0 + scale) + shift
        j = i // N_MIXERS
        if i % N_MIXERS == 0:
            y = stick_breaking_mixer(h, sb_w_qkv[j], sb_w_o[j])
        else:
            y = token_mixing_mixer(h, tm_w_in[j], tm_ln_g[j], tm_w_s[j], tm_b_s[j], tm_w_out[j])
        x = x + gate * rms_norm(y, mix_post_g[i])
        shift, scale, gate = ada_modulation(c, ffn_ada_w[i], ffn_ada_b[i])
        h = rms_norm(x, ffn_pre_g[i]) * (1.0 + scale) + shift
        y = swiglu(h, ffn_w_gate[i], ffn_w_up[i], ffn_w_down[i])
        x = x + gate * rms_norm(y, ffn_post_g[i])
    return x
```

```python
import functools

import jax
import jax.numpy as jnp
from jax import lax
from jax.experimental import pallas as pl
from jax.experimental.pallas import tpu as pltpu

EPS = 1e-6
CHUNK = 64
SB_HEAD_DIM = 64
TM_CHUNK = 128
LANES = 128
ATTN_TILE = 256
VMEM_LIMIT_BYTES = 56 * 1024 * 1024

F32 = jnp.float32
BF16 = jnp.bfloat16
NEG_BIG = -1e30


def _params(n_axes):
    return pltpu.CompilerParams(
        dimension_semantics=("parallel",) * n_axes,
        vmem_limit_bytes=VMEM_LIMIT_BYTES,
    )


def _resident(block_shape, index_map):
    return pl.BlockSpec(block_shape, index_map, pipeline_mode=pl.Buffered(1))


def _rms(x, g):
    return x * lax.rsqrt(jnp.mean(x * x, axis=-1, keepdims=True) + EPS) * g


def _modulated_norm(x, g, mod):
    return _rms(x, g) * (1.0 + mod[1:2]) + mod[0:1]


def _dot(a, b):
    return jnp.dot(a, b, preferred_element_type=F32)


def _ada_kernel(c_ref, mw_ref, mb_ref, fw_ref, fb_ref, mo_ref, fo_ref):
    s = jax.nn.silu(c_ref[...]).astype(BF16)
    mo_ref[0] = _dot(s, mw_ref[0].astype(BF16)) + mb_ref[0]
    fo_ref[0] = _dot(s, fw_ref[0].astype(BF16)) + fb_ref[0]


def _ada(c, mix_w, mix_b, ffn_w, ffn_b):
    depth, d, d3 = mix_w.shape
    b = c.shape[0]
    nblk = d3 // d
    w_spec = pl.BlockSpec((1, d, d), lambda l, n: (l, 0, n))
    b_spec = pl.BlockSpec((1, 1, d), lambda l, n: (l, 0, n))
    o_spec = pl.BlockSpec((1, b, d), lambda l, n: (l, 0, n))
    out = jax.ShapeDtypeStruct((depth, b, d3), F32)
    return pl.pallas_call(
        _ada_kernel,
        grid=(depth, nblk),
        in_specs=[pl.BlockSpec((b, d), lambda l, n: (0, 0)), w_spec, b_spec, w_spec, b_spec],
        out_specs=[o_spec, o_spec],
        out_shape=[out, out],
        compiler_params=_params(2),
        name="ada",
    )(c, mix_w, mix_b.reshape(depth, 1, d3), ffn_w, ffn_b.reshape(depth, 1, d3))


def _qkv_kernel(x_ref, mod_ref, g_ref, w_ref, o_ref, *, d, q_scale):
    h = _modulated_norm(x_ref[0], g_ref[...], mod_ref[0]).astype(BF16)
    acc = _dot(h, w_ref[...])
    o_ref[0, :, :d] = (acc[:, :d] * q_scale).astype(BF16)
    o_ref[0, :, d:] = acc[:, d:].astype(BF16)


def _qkv(x, mod, g, w, tm):
    b, s, d = x.shape
    n = w.shape[1]
    return pl.pallas_call(
        functools.partial(_qkv_kernel, d=d, q_scale=SB_HEAD_DIM ** -0.5),
        grid=(b, s // tm),
        in_specs=[
            pl.BlockSpec((1, tm, d), lambda i, j: (i, j, 0)),
            pl.BlockSpec((1, 3, d), lambda i, j: (i, 0, 0)),
            _resident((1, d), lambda i, j: (0, 0)),
            _resident((d, n), lambda i, j: (0, 0)),
        ],
        out_specs=pl.BlockSpec((1, tm, n), lambda i, j: (i, j, 0)),
        out_shape=jax.ShapeDtypeStruct((b, s, n), BF16),
        compiler_params=_params(2),
        name="qkv",
    )(x, mod, g, w)


def _sb_tile(qh, k_t, v_t, tri, carry, diag_mask):
    z = lax.dot_general(qh, k_t, (((1,), (1,)), ((), ())), preferred_element_type=F32)
    if diag_mask is not None:
        z = jnp.where(diag_mask, z, NEG_BIG)
    l1p = jnp.log1p(jnp.exp(-jnp.abs(z)))
    sp = jnp.maximum(z, 0.0) + l1p
    lb = jnp.minimum(z, 0.0) - l1p
    later = _dot(sp.astype(BF16), tri)
    a = jnp.exp(lb - (later + carry))
    o = _dot(a.astype(BF16), v_t)
    carry = carry + later[:, 0:1] + sp[:, 0:1]
    return o, carry


def _sb_attn_kernel(q_ref, k_ref, v_ref, o_ref, *, seq, tile):
    lane = lax.broadcasted_iota(jnp.int32, (tile, LANES), 1)
    first_head = lane < SB_HEAD_DIM
    row = lax.broadcasted_iota(jnp.int32, (tile, tile), 0)
    col = lax.broadcasted_iota(jnp.int32, (tile, tile), 1)
    causal = col < row
    tri = (row > col).astype(BF16)
    for i in range(seq // tile):
        q_i = q_ref[0, i * tile:(i + 1) * tile, :]
        heads = []
        for qh in (jnp.where(first_head, q_i, 0), jnp.where(first_head, 0, q_i)):
            carry = jnp.zeros((tile, 1), F32)
            acc = jnp.zeros((tile, LANES), F32)
            for j in range(i, -1, -1):
                k_t = k_ref[0, j * tile:(j + 1) * tile, :]
                v_t = v_ref[0, j * tile:(j + 1) * tile, :]
                o, carry = _sb_tile(qh, k_t, v_t, tri, carry, causal if j == i else None)
                acc = acc + o
            heads.append(acc)
        o_ref[0, i * tile:(i + 1) * tile, :] = jnp.where(first_head, heads[0], heads[1]).astype(BF16)


def _sb_attn(qkv, d):
    b, s, _ = qkv.shape
    pairs = d // LANES

    def spec(offset):
        return pl.BlockSpec((1, s, LANES), lambda i, p: (i, 0, p + offset))

    return pl.pallas_call(
        functools.partial(_sb_attn_kernel, seq=s, tile=ATTN_TILE),
        grid=(b, pairs),
        in_specs=[spec(0), spec(pairs), spec(2 * pairs)],
        out_specs=pl.BlockSpec((1, s, LANES), lambda i, p: (i, 0, p)),
        out_shape=jax.ShapeDtypeStruct((b, s, d), BF16),
        compiler_params=_params(2),
        name="sb_attn",
    )(qkv, qkv, qkv)


def _out_proj_kernel(a_ref, x_ref, mod_ref, g_ref, w_ref, o_ref):
    y = _dot(a_ref[0], w_ref[...])
    o_ref[0] = x_ref[0] + mod_ref[0][2:3] * _rms(y, g_ref[...])


def _out_proj(a, x, mod, g, w, tm):
    b, s, d = x.shape
    row = pl.BlockSpec((1, tm, d), lambda i, j: (i, j, 0))
    return pl.pallas_call(
        _out_proj_kernel,
        grid=(b, s // tm),
        in_specs=[
            row, row,
            pl.BlockSpec((1, 3, d), lambda i, j: (i, 0, 0)),
            _resident((1, d), lambda i, j: (0, 0)),
            _resident((d, d), lambda i, j: (0, 0)),
        ],
        out_specs=row,
        out_shape=jax.ShapeDtypeStruct((b, s, d), F32),
        compiler_params=_params(2),
        name="out_proj",
    )(a, x, mod, g, w)


def _ffn_kernel(x_ref, mod_ref, gpre_ref, gpost_ref, wg_ref, wu_ref, wd_ref, o_ref):
    x = x_ref[0]
    mod = mod_ref[0]
    h = _modulated_norm(x, gpre_ref[...], mod).astype(BF16)
    act = (jax.nn.silu(_dot(h, wg_ref[...])) * _dot(h, wu_ref[...])).astype(BF16)
    y = _dot(act, wd_ref[...])
    o_ref[0] = x + mod[2:3] * _rms(y, gpost_ref[...])


def _ffn(x, mod, gpre, gpost, wg, wu, wd, tm):
    b, s, d = x.shape
    dff = wg.shape[1]
    row = pl.BlockSpec((1, tm, d), lambda i, j: (i, j, 0))
    vec = _resident((1, d), lambda i, j: (0, 0))
    return pl.pallas_call(
        _ffn_kernel,
        grid=(b, s // tm),
        in_specs=[
            row,
            pl.BlockSpec((1, 3, d), lambda i, j: (i, 0, 0)),
            vec, vec,
            _resident((d, dff), lambda i, j: (0, 0)),
            _resident((d, dff), lambda i, j: (0, 0)),
            _resident((dff, d), lambda i, j: (0, 0)),
        ],
        out_specs=row,
        out_shape=jax.ShapeDtypeStruct((b, s, d), F32),
        compiler_params=_params(2),
        name="ffn",
    )(x, mod, gpre, gpost, wg, wu, wd)


def _token_mixer_kernel(x_ref, mod_ref, gpre_ref, gpost_ref, win_ref, lng_ref, ws_ref, bs_ref, wout_ref,
                        o_ref, *, d, tm, groups):
    x = x_ref[0]
    mod = mod_ref[0]
    h = _modulated_norm(x, gpre_ref[...], mod).astype(BF16)
    uv = jax.nn.gelu(_dot(h, win_ref[...]))
    u = uv[:, :d]
    v = uv[:, d:]
    vc = v - jnp.mean(v, axis=-1, keepdims=True)
    v = (vc * lax.rsqrt(jnp.mean(vc * vc, axis=-1, keepdims=True) + EPS) * lng_ref[...]).astype(BF16)

    pos_t = lax.broadcasted_iota(jnp.int32, (TM_CHUNK, TM_CHUNK), 0) // CHUNK
    pos_s = lax.broadcasted_iota(jnp.int32, (TM_CHUNK, TM_CHUNK), 1) // CHUNK
    chunk_causal = pos_t >= pos_s
    gd = d // groups
    bias = bs_ref[...]
    rows = []
    for n in range(tm // TM_CHUNK):
        cols = []
        for g in range(groups):
            w_g = jnp.where(chunk_causal, ws_ref[g], 0.0).astype(BF16)
            v_ng = v[n * TM_CHUNK:(n + 1) * TM_CHUNK, g * gd:(g + 1) * gd]
            cols.append(_dot(w_g, v_ng) + bias[:, g:g + 1])
        rows.append(jnp.concatenate(cols, axis=1))
    sv = jnp.concatenate(rows, axis=0) if len(rows) > 1 else rows[0]
    y = _dot((u * sv).astype(BF16), wout_ref[...])
    o_ref[0] = x + mod[2:3] * _rms(y, gpost_ref[...])


def _token_mixer(x, mod, gpre, gpost, w_in, ln_g, w_s, b_s_t, w_out, tm):
    b, s, d = x.shape
    groups, t, _ = w_s.shape
    row = pl.BlockSpec((1, tm, d), lambda i, j: (i, j, 0))
    vec = _resident((1, d), lambda i, j: (0, 0))
    return pl.pallas_call(
        functools.partial(_token_mixer_kernel, d=d, tm=tm, groups=groups),
        grid=(b, s // tm),
        in_specs=[
            row,
            pl.BlockSpec((1, 3, d), lambda i, j: (i, 0, 0)),
            vec, vec,
            _resident((d, 2 * d), lambda i, j: (0, 0)),
            vec,
            _resident((groups, t, t), lambda i, j: (0, 0, 0)),
            _resident((t, groups), lambda i, j: (0, 0)),
            _resident((d, d), lambda i, j: (0, 0)),
        ],
        out_specs=row,
        out_shape=jax.ShapeDtypeStruct((b, s, d), F32),
        compiler_params=_params(2),
        name="token_mixer",
    )(x, mod, gpre, gpost, w_in, ln_g, w_s, b_s_t, w_out)


def kernel(x, c, mix_ada_w, mix_ada_b, mix_pre_g, mix_post_g, sb_w_qkv, sb_w_o, tm_w_in, tm_ln_g, tm_w_s, tm_b_s, tm_w_out, ffn_ada_w, ffn_ada_b, ffn_pre_g, ffn_post_g, ffn_w_gate, ffn_w_up, ffn_w_down):
    b, s, d = x.shape
    depth = mix_ada_w.shape[0]
    assert depth == 2 and s % ATTN_TILE == 0 and d % LANES == 0

    mix_mod, ffn_mod = _ada(c, mix_ada_w, mix_ada_b, ffn_ada_w, ffn_ada_b)
    mix_mod = mix_mod.reshape(depth, b, 3, d)
    ffn_mod = ffn_mod.reshape(depth, b, 3, d)

    def vec(a, i):
        return a[i].reshape(1, d)

    def ffn_layer(x, i):
        return _ffn(x, ffn_mod[i], vec(ffn_pre_g, i), vec(ffn_post_g, i), ffn_w_gate[i].astype(BF16),
                    ffn_w_up[i].astype(BF16), ffn_w_down[i].astype(BF16), tm=256)

    qkv = _qkv(x, mix_mod[0], vec(mix_pre_g, 0), sb_w_qkv[0].astype(BF16), tm=512)
    attn = _sb_attn(qkv, d)
    x = _out_proj(attn, x, mix_mod[0], vec(mix_post_g, 0), sb_w_o[0].astype(BF16), tm=512)
    x = ffn_layer(x, 0)

    x = _token_mixer(x, mix_mod[1], vec(mix_pre_g, 1), vec(mix_post_g, 1), tm_w_in[0].astype(BF16),
                     vec(tm_ln_g, 0), tm_w_s[0], tm_b_s[0].T, tm_w_out[0].astype(BF16), tm=256)
    x = ffn_layer(x, 1)
    return x
```

```python
import functools
import math

import jax
import jax.numpy as jnp
from jax import lax
from jax.experimental import pallas as pl
from jax.experimental.pallas import tpu as pltpu

EPS = 1e-6
CHUNK = 64
SB_HEAD_DIM = 64
TM_CHUNK = 128
LANES = 128
ATTN_TILE = 256
VMEM_LIMIT_BYTES = 56 * 1024 * 1024

F32 = jnp.float32
BF16 = jnp.bfloat16
NEG_BIG = -1e30


def _params(n_axes):
    return pltpu.CompilerParams(
        dimension_semantics=("parallel",) * n_axes,
        vmem_limit_bytes=VMEM_LIMIT_BYTES,
    )


def _resident(block_shape, index_map):
    return pl.BlockSpec(block_shape, index_map, pipeline_mode=pl.Buffered(1))


def _rms(x, g):
    return x * lax.rsqrt(jnp.mean(x * x, axis=-1, keepdims=True) + EPS) * g


def _modulated_norm(x, g, mod):
    return _rms(x, g) * (1.0 + mod[1:2]) + mod[0:1]


def _dot(a, b):
    return jnp.dot(a, b, preferred_element_type=F32)


def _ada_kernel(c_ref, mw_ref, mb_ref, fw_ref, fb_ref, mo_ref, fo_ref):
    s = jax.nn.silu(c_ref[...]).astype(BF16)
    mo_ref[0] = _dot(s, mw_ref[0].astype(BF16)) + mb_ref[0]
    fo_ref[0] = _dot(s, fw_ref[0].astype(BF16)) + fb_ref[0]


def _ada(c, mix_w, mix_b, ffn_w, ffn_b):
    depth, d, d3 = mix_w.shape
    b = c.shape[0]
    nblk = d3 // d
    w_spec = pl.BlockSpec((1, d, d), lambda l, n: (l, 0, n))
    b_spec = pl.BlockSpec((1, 1, d), lambda l, n: (l, 0, n))
    o_spec = pl.BlockSpec((1, b, d), lambda l, n: (l, 0, n))
    out = jax.ShapeDtypeStruct((depth, b, d3), F32)
    return pl.pallas_call(
        _ada_kernel,
        grid=(depth, nblk),
        in_specs=[pl.BlockSpec((b, d), lambda l, n: (0, 0)), w_spec, b_spec, w_spec, b_spec],
        out_specs=[o_spec, o_spec],
        out_shape=[out, out],
        compiler_params=_params(2),
        name="ada",
    )(c, mix_w, mix_b.reshape(depth, 1, d3), ffn_w, ffn_b.reshape(depth, 1, d3))


def _qkv_kernel(x_ref, mod_ref, g_ref, w_ref, o_ref, *, d, q_scale):
    h = _modulated_norm(x_ref[0], g_ref[...], mod_ref[0]).astype(BF16)
    acc = _dot(h, w_ref[...])
    o_ref[0, :, :d] = (acc[:, :d] * q_scale).astype(BF16)
    o_ref[0, :, d:] = acc[:, d:].astype(BF16)


def _qkv(x, mod, g, w, tm):
    b, s, d = x.shape
    n = w.shape[1]
    return pl.pallas_call(
        functools.partial(_qkv_kernel, d=d, q_scale=SB_HEAD_DIM ** -0.5 * math.log2(math.e)),
        grid=(b, s // tm),
        in_specs=[
            pl.BlockSpec((1, tm, d), lambda i, j: (i, j, 0)),
            pl.BlockSpec((1, 3, d), lambda i, j: (i, 0, 0)),
            _resident((1, d), lambda i, j: (0, 0)),
            _resident((d, n), lambda i, j: (0, 0)),
        ],
        out_specs=pl.BlockSpec((1, tm, n), lambda i, j: (i, j, 0)),
        out_shape=jax.ShapeDtypeStruct((b, s, n), BF16),
        compiler_params=_params(2),
        name="qkv",
    )(x, mod, g, w)


def _sb_tile(qh, k_t, v_t, tri, diag_mask, tile):
    z = lax.dot_general(qh, k_t, (((1,), (1,)), ((), ())), preferred_element_type=F32)
    if diag_mask is not None:
        z = jnp.where(diag_mask, z, NEG_BIG)
    neg_abs = lax.bitcast_convert_type(lax.bitcast_convert_type(z, jnp.uint32) | jnp.uint32(0x80000000), F32)
    sp = jnp.maximum(z, 0.0) + jnp.log2(1.0 + jnp.exp2(neg_abs))
    later = _dot(sp.astype(BF16), tri)
    a = jnp.exp2((z - sp) - later)
    o = _dot(a.astype(BF16), v_t)
    totals = [later[r:r + tile, 0:1] + sp[r:r + tile, 0:1] for r in range(0, z.shape[0], tile)]
    return o, totals


def _sb_attn_kernel(q_ref, k_ref, v_ref, o_ref, *, seq, tile):
    nsub = seq // tile
    first_head = lax.broadcasted_iota(jnp.int32, (seq, LANES), 1) < SB_HEAD_DIM
    first_head_tile = lax.broadcasted_iota(jnp.int32, (tile, LANES), 1) < SB_HEAD_DIM
    row = lax.broadcasted_iota(jnp.int32, (seq, tile), 0)
    col = lax.broadcasted_iota(jnp.int32, (seq, tile), 1)
    causal = (row >= tile) | (col < row)
    trow = lax.broadcasted_iota(jnp.int32, (tile, tile), 0)
    tcol = lax.broadcasted_iota(jnp.int32, (tile, tile), 1)
    tri = (trow > tcol).astype(BF16)
    q = q_ref[0]
    heads = []
    for qh in (jnp.where(first_head, q, 0), jnp.where(first_head, 0, q)):
        acc = [jnp.zeros((tile, LANES), F32) for _ in range(nsub)]
        carry = [jnp.zeros((tile, 1), F32) for _ in range(nsub)]
        for j in range(nsub - 1, -1, -1):
            k_t = k_ref[0, j * tile:(j + 1) * tile, :]
            v_t = v_ref[0, j * tile:(j + 1) * tile, :]
            o, totals = _sb_tile(qh[j * tile:], k_t, v_t, tri, causal[:(nsub - j) * tile], tile)
            for r in range(j, nsub):
                acc[r] = acc[r] + o[(r - j) * tile:(r - j + 1) * tile] * jnp.exp2(-carry[r])
                carry[r] = carry[r] + totals[r - j]
        heads.append(acc)
    for r in range(nsub):
        o_ref[0, r * tile:(r + 1) * tile, :] = jnp.where(first_head_tile, heads[0][r], heads[1][r]).astype(BF16)


def _sb_attn(qkv, d):
    b, s, _ = qkv.shape
    pairs = d // LANES

    def spec(offset):
        return pl.BlockSpec((1, s, LANES), lambda i, p: (i, 0, p + offset))

    return pl.pallas_call(
        functools.partial(_sb_attn_kernel, seq=s, tile=ATTN_TILE),
        grid=(b, pairs),
        in_specs=[spec(0), spec(pairs), spec(2 * pairs)],
        out_specs=pl.BlockSpec((1, s, LANES), lambda i, p: (i, 0, p)),
        out_shape=jax.ShapeDtypeStruct((b, s, d), BF16),
        compiler_params=_params(2),
        name="sb_attn",
    )(qkv, qkv, qkv)


def _out_proj_kernel(a_ref, x_ref, mod_ref, g_ref, w_ref, o_ref):
    y = _dot(a_ref[0], w_ref[...])
    o_ref[0] = x_ref[0] + mod_ref[0][2:3] * _rms(y, g_ref[...])


def _out_proj(a, x, mod, g, w, tm):
    b, s, d = x.shape
    row = pl.BlockSpec((1, tm, d), lambda i, j: (i, j, 0))
    return pl.pallas_call(
        _out_proj_kernel,
        grid=(b, s // tm),
        in_specs=[
            row, row,
            pl.BlockSpec((1, 3, d), lambda i, j: (i, 0, 0)),
            _resident((1, d), lambda i, j: (0, 0)),
            _resident((d, d), lambda i, j: (0, 0)),
        ],
        out_specs=row,
        out_shape=jax.ShapeDtypeStruct((b, s, d), F32),
        compiler_params=_params(2),
        name="out_proj",
    )(a, x, mod, g, w)


def _ffn_kernel(x_ref, mod_ref, gpre_ref, gpost_ref, wg_ref, wu_ref, wd_ref, o_ref):
    x = x_ref[0]
    mod = mod_ref[0]
    h = _modulated_norm(x, gpre_ref[...], mod).astype(BF16)
    act = (jax.nn.silu(_dot(h, wg_ref[...])) * _dot(h, wu_ref[...])).astype(BF16)
    y = _dot(act, wd_ref[...])
    o_ref[0] = x + mod[2:3] * _rms(y, gpost_ref[...])


def _ffn(x, mod, gpre, gpost, wg, wu, wd, tm):
    b, s, d = x.shape
    dff = wg.shape[1]
    row = pl.BlockSpec((1, tm, d), lambda i, j: (i, j, 0))
    vec = _resident((1, d), lambda i, j: (0, 0))
    return pl.pallas_call(
        _ffn_kernel,
        grid=(b, s // tm),
        in_specs=[
            row,
            pl.BlockSpec((1, 3, d), lambda i, j: (i, 0, 0)),
            vec, vec,
            _resident((d, dff), lambda i, j: (0, 0)),
            _resident((d, dff), lambda i, j: (0, 0)),
            _resident((dff, d), lambda i, j: (0, 0)),
        ],
        out_specs=row,
        out_shape=jax.ShapeDtypeStruct((b, s, d), F32),
        compiler_params=_params(2),
        name="ffn",
    )(x, mod, gpre, gpost, wg, wu, wd)


def _token_mixer_kernel(x_ref, mod_ref, gpre_ref, gpost_ref, win_ref, lng_ref, ws_ref, bs_ref, wout_ref,
                        o_ref, *, d, tm, groups):
    x = x_ref[0]
    mod = mod_ref[0]
    h = _modulated_norm(x, gpre_ref[...], mod).astype(BF16)
    uv = jax.nn.gelu(_dot(h, win_ref[...]))
    u = uv[:, :d]
    v = uv[:, d:]
    vc = v - jnp.mean(v, axis=-1, keepdims=True)
    v = (vc * lax.rsqrt(jnp.mean(vc * vc, axis=-1, keepdims=True) + EPS) * lng_ref[...]).astype(BF16)

    pos_t = lax.broadcasted_iota(jnp.int32, (TM_CHUNK, TM_CHUNK), 0) // CHUNK
    pos_s = lax.broadcasted_iota(jnp.int32, (TM_CHUNK, TM_CHUNK), 1) // CHUNK
    chunk_causal = pos_t >= pos_s
    gd = d // groups
    bias = bs_ref[...]
    rows = []
    for n in range(tm // TM_CHUNK):
        cols = []
        for g in range(groups):
            w_g = jnp.where(chunk_causal, ws_ref[g], 0.0).astype(BF16)
            v_ng = v[n * TM_CHUNK:(n + 1) * TM_CHUNK, g * gd:(g + 1) * gd]
            cols.append(_dot(w_g, v_ng) + bias[:, g:g + 1])
        rows.append(jnp.concatenate(cols, axis=1))
    sv = jnp.concatenate(rows, axis=0) if len(rows) > 1 else rows[0]
    y = _dot((u * sv).astype(BF16), wout_ref[...])
    o_ref[0] = x + mod[2:3] * _rms(y, gpost_ref[...])


def _token_mixer(x, mod, gpre, gpost, w_in, ln_g, w_s, b_s_t, w_out, tm):
    b, s, d = x.shape
    groups, t, _ = w_s.shape
    row = pl.BlockSpec((1, tm, d), lambda i, j: (i, j, 0))
    vec = _resident((1, d), lambda i, j: (0, 0))
    return pl.pallas_call(
        functools.partial(_token_mixer_kernel, d=d, tm=tm, groups=groups),
        grid=(b, s // tm),
        in_specs=[
            row,
            pl.BlockSpec((1, 3, d), lambda i, j: (i, 0, 0)),
            vec, vec,
            _resident((d, 2 * d), lambda i, j: (0, 0)),
            vec,
            _resident((groups, t, t), lambda i, j: (0, 0, 0)),
            _resident((t, groups), lambda i, j: (0, 0)),
            _resident((d, d), lambda i, j: (0, 0)),
        ],
        out_specs=row,
        out_shape=jax.ShapeDtypeStruct((b, s, d), F32),
        compiler_params=_params(2),
        name="token_mixer",
    )(x, mod, gpre, gpost, w_in, ln_g, w_s, b_s_t, w_out)


def kernel(x, c, mix_ada_w, mix_ada_b, mix_pre_g, mix_post_g, sb_w_qkv, sb_w_o, tm_w_in, tm_ln_g, tm_w_s, tm_b_s, tm_w_out, ffn_ada_w, ffn_ada_b, ffn_pre_g, ffn_post_g, ffn_w_gate, ffn_w_up, ffn_w_down):
    b, s, d = x.shape
    depth = mix_ada_w.shape[0]
    assert depth == 2 and s % ATTN_TILE == 0 and d % LANES == 0

    mix_mod, ffn_mod = _ada(c, mix_ada_w, mix_ada_b, ffn_ada_w, ffn_ada_b)
    mix_mod = mix_mod.reshape(depth, b, 3, d)
    ffn_mod = ffn_mod.reshape(depth, b, 3, d)

    def vec(a, i):
        return a[i].reshape(1, d)

    def ffn_layer(x, i):
        return _ffn(x, ffn_mod[i], vec(ffn_pre_g, i), vec(ffn_post_g, i), ffn_w_gate[i].astype(BF16),
                    ffn_w_up[i].astype(BF16), ffn_w_down[i].astype(BF16), tm=512)

    qkv = _qkv(x, mix_mod[0], vec(mix_pre_g, 0), sb_w_qkv[0].astype(BF16), tm=512)
    attn = _sb_attn(qkv, d)
    x = _out_proj(attn, x, mix_mod[0], vec(mix_post_g, 0), sb_w_o[0].astype(BF16), tm=1024)
    x = ffn_layer(x, 0)

    x = _token_mixer(x, mix_mod[1], vec(mix_pre_g, 1), vec(mix_post_g, 1), tm_w_in[0].astype(BF16),
                     vec(tm_ln_g, 0), tm_w_s[0], tm_b_s[0].T, tm_w_out[0].astype(BF16), tm=512)
    x = ffn_layer(x, 1)
    return x
```

```python
import functools
import math

import jax
import jax.numpy as jnp
from jax import lax
from jax.experimental import pallas as pl
from jax.experimental.pallas import tpu as pltpu

EPS = 1e-6
CHUNK = 64
SB_HEAD_DIM = 64
TM_CHUNK = 128
LANES = 128
ATTN_TILE = 256
FFN_ROW_CHUNK = 128
NEAR_BLOCKS = 2
FAR_SKIP_LOG2 = 160.0
VMEM_LIMIT_BYTES = 56 * 1024 * 1024

F32 = jnp.float32
BF16 = jnp.bfloat16
NEG_BIG = -1e30


def _params(n_axes):
    return pltpu.CompilerParams(
        dimension_semantics=("parallel",) * n_axes,
        vmem_limit_bytes=VMEM_LIMIT_BYTES,
    )


def _resident(block_shape, index_map):
    return pl.BlockSpec(block_shape, index_map, pipeline_mode=pl.Buffered(1))


def _rms(x, g):
    return x * lax.rsqrt(jnp.mean(x * x, axis=-1, keepdims=True) + EPS) * g


def _modulated_norm(x, g, mod):
    return _rms(x, g) * (1.0 + mod[1:2]) + mod[0:1]


def _dot(a, b):
    return jnp.dot(a, b, preferred_element_type=F32)


def _ada_kernel(c_ref, mw_ref, mb_ref, fw_ref, fb_ref, mo_ref, fo_ref):
    s = jax.nn.silu(c_ref[...]).astype(BF16)
    mo_ref[0] = _dot(s, mw_ref[0].astype(BF16)) + mb_ref[0]
    fo_ref[0] = _dot(s, fw_ref[0].astype(BF16)) + fb_ref[0]


def _ada(c, mix_w, mix_b, ffn_w, ffn_b):
    depth, d, d3 = mix_w.shape
    b = c.shape[0]
    nblk = d3 // d
    w_spec = pl.BlockSpec((1, d, d), lambda l, n: (l, 0, n))
    b_spec = pl.BlockSpec((1, 1, d), lambda l, n: (l, 0, n))
    o_spec = pl.BlockSpec((1, b, d), lambda l, n: (l, 0, n))
    out = jax.ShapeDtypeStruct((depth, b, d3), F32)
    return pl.pallas_call(
        _ada_kernel,
        grid=(depth, nblk),
        in_specs=[pl.BlockSpec((b, d), lambda l, n: (0, 0)), w_spec, b_spec, w_spec, b_spec],
        out_specs=[o_spec, o_spec],
        out_shape=[out, out],
        compiler_params=_params(2),
        name="ada",
    )(c, mix_w, mix_b.reshape(depth, 1, d3), ffn_w, ffn_b.reshape(depth, 1, d3))


def _qkv_kernel(x_ref, mod_ref, g_ref, w_ref, o_ref, *, d, q_scale):
    h = _modulated_norm(x_ref[0], g_ref[...], mod_ref[0]).astype(BF16)
    acc = _dot(h, w_ref[...])
    o_ref[0, :, :d] = (acc[:, :d] * q_scale).astype(BF16)
    o_ref[0, :, d:] = acc[:, d:].astype(BF16)


def _qkv(x, mod, g, w, tm):
    b, s, d = x.shape
    n = w.shape[1]
    return pl.pallas_call(
        functools.partial(_qkv_kernel, d=d, q_scale=SB_HEAD_DIM ** -0.5 * math.log2(math.e)),
        grid=(b, s // tm),
        in_specs=[
            pl.BlockSpec((1, tm, d), lambda i, j: (i, j, 0)),
            pl.BlockSpec((1, 3, d), lambda i, j: (i, 0, 0)),
            _resident((1, d), lambda i, j: (0, 0)),
            _resident((d, n), lambda i, j: (0, 0)),
        ],
        out_specs=pl.BlockSpec((1, tm, n), lambda i, j: (i, j, 0)),
        out_shape=jax.ShapeDtypeStruct((b, s, n), BF16),
        compiler_params=_params(2),
        name="qkv",
    )(x, mod, g, w)


def _sb_tile(qh, k_t, v_t, tri, diag_mask, tile):
    z = lax.dot_general(qh, k_t, (((1,), (1,)), ((), ())), preferred_element_type=F32)
    if diag_mask is not None:
        z = jnp.where(diag_mask, z, NEG_BIG)
    neg_abs = lax.bitcast_convert_type(lax.bitcast_convert_type(z, jnp.uint32) | jnp.uint32(0x80000000), F32)
    sp = jnp.maximum(z, 0.0) + jnp.log2(1.0 + jnp.exp2(neg_abs))
    later = _dot(sp.astype(BF16), tri)
    a = jnp.exp2((z - sp) - later)
    o = _dot(a.astype(BF16), v_t)
    totals = [jnp.broadcast_to(later[r:r + tile, 0:1] + sp[r:r + tile, 0:1], (tile, LANES))
              for r in range(0, z.shape[0], tile)]
    return o, totals


def _head_lanes(rows, head):
    lane = lax.broadcasted_iota(jnp.int32, (rows, LANES), 1)
    return (lane < SB_HEAD_DIM) if head == 0 else (lane >= SB_HEAD_DIM)


def _sb_attn_kernel(q_ref, k_ref, v_ref, o_ref, acc_ref, carry_ref, *, seq, tile):
    nsub = seq // tile
    row = lax.broadcasted_iota(jnp.int32, (NEAR_BLOCKS * tile, tile), 0)
    col = lax.broadcasted_iota(jnp.int32, (NEAR_BLOCKS * tile, tile), 1)
    causal = (row >= tile) | (col < row)
    trow = lax.broadcasted_iota(jnp.int32, (tile, tile), 0)
    tcol = lax.broadcasted_iota(jnp.int32, (tile, tile), 1)
    tri = (trow > tcol).astype(BF16)
    causal_last = tcol < trow

    def q_rows(head, lo, hi):
        return jnp.where(_head_lanes(hi - lo, head), q_ref[0, lo:hi, :], 0)

    far_needed = []
    for head in range(2):
        acc = [None] * nsub
        carry = [None] * nsub
        for j in range(nsub - 1, -1, -1):
            nb = min(NEAR_BLOCKS, nsub - j)
            k_t = k_ref[0, j * tile:(j + 1) * tile, :]
            v_t = v_ref[0, j * tile:(j + 1) * tile, :]
            o, totals = _sb_tile(q_rows(head, j * tile, (j + nb) * tile), k_t, v_t, tri,
                                 causal if nb == NEAR_BLOCKS else causal_last, tile)
            acc[j], carry[j] = o[:tile], totals[0]
            for b in range(1, nb):
                r = j + b
                acc[r] = acc[r] + o[b * tile:(b + 1) * tile] * jnp.exp2(-carry[r])
                carry[r] = carry[r] + totals[b]
        for r in range(nsub):
            acc_ref[head, r * tile:(r + 1) * tile, :] = acc[r]
            carry_ref[head, r * tile:(r + 1) * tile, :] = carry[r]
        low = None
        flags = [None] * nsub
        for r in range(nsub - 1, NEAR_BLOCKS - 1, -1):
            m = jnp.min(carry[r])
            low = m if low is None else jnp.minimum(low, m)
            flags[r] = low < FAR_SKIP_LOG2
        far_needed.append(flags)

    for head in range(2):
        for j in range(nsub - NEAR_BLOCKS - 1, -1, -1):
            lo = (j + NEAR_BLOCKS) * tile

            @pl.when(far_needed[head][j + NEAR_BLOCKS])
            def _(head=head, j=j, lo=lo):
                k_t = k_ref[0, j * tile:(j + 1) * tile, :]
                v_t = v_ref[0, j * tile:(j + 1) * tile, :]
                o, totals = _sb_tile(q_rows(head, lo, seq), k_t, v_t, tri, None, tile)
                for b in range(nsub - j - NEAR_BLOCKS):
                    rows = slice(lo + b * tile, lo + (b + 1) * tile)
                    c = carry_ref[head, rows, :]
                    acc_ref[head, rows, :] = acc_ref[head, rows, :] + o[b * tile:(b + 1) * tile] * jnp.exp2(-c)
                    carry_ref[head, rows, :] = c + totals[b]

    o_ref[0] = jnp.where(_head_lanes(seq, 0), acc_ref[0], acc_ref[1]).astype(BF16)


def _sb_attn(qkv, d):
    b, s, _ = qkv.shape
    pairs = d // LANES

    def spec(offset):
        return pl.BlockSpec((1, s, LANES), lambda i, p: (i, 0, p + offset))

    return pl.pallas_call(
        functools.partial(_sb_attn_kernel, seq=s, tile=ATTN_TILE),
        grid=(b, pairs),
        in_specs=[spec(0), spec(pairs), spec(2 * pairs)],
        out_specs=pl.BlockSpec((1, s, LANES), lambda i, p: (i, 0, p)),
        out_shape=jax.ShapeDtypeStruct((b, s, d), BF16),
        scratch_shapes=[pltpu.VMEM((2, s, LANES), F32), pltpu.VMEM((2, s, LANES), F32)],
        compiler_params=_params(2),
        name="sb_attn",
    )(qkv, qkv, qkv)


def _out_proj_kernel(a_ref, x_ref, mod_ref, g_ref, w_ref, o_ref):
    y = _dot(a_ref[0], w_ref[...])
    o_ref[0] = x_ref[0] + mod_ref[0][2:3] * _rms(y, g_ref[...])


def _out_proj(a, x, mod, g, w, tm):
    b, s, d = x.shape
    row = pl.BlockSpec((1, tm, d), lambda i, j: (i, j, 0))
    return pl.pallas_call(
        _out_proj_kernel,
        grid=(b, s // tm),
        in_specs=[
            row, row,
            pl.BlockSpec((1, 3, d), lambda i, j: (i, 0, 0)),
            _resident((1, d), lambda i, j: (0, 0)),
            _resident((d, d), lambda i, j: (0, 0)),
        ],
        out_specs=row,
        out_shape=jax.ShapeDtypeStruct((b, s, d), F32),
        compiler_params=_params(2),
        name="out_proj",
    )(a, x, mod, g, w)


def _ffn_kernel(x_ref, mod_ref, gpre_ref, gpost_ref, wg_ref, wu_ref, wd_ref, o_ref, *, tm, chunk):
    mod = mod_ref[0]
    for r in range(0, tm, chunk):
        x = x_ref[0, r:r + chunk]
        h = _modulated_norm(x, gpre_ref[...], mod).astype(BF16)
        act = (jax.nn.silu(_dot(h, wg_ref[...])) * _dot(h, wu_ref[...])).astype(BF16)
        y = _dot(act, wd_ref[...])
        o_ref[0, r:r + chunk] = x + mod[2:3] * _rms(y, gpost_ref[...])


def _ffn(x, mod, gpre, gpost, wg, wu, wd, tm):
    b, s, d = x.shape
    dff = wg.shape[1]
    row = pl.BlockSpec((1, tm, d), lambda i, j: (i, j, 0))
    vec = _resident((1, d), lambda i, j: (0, 0))
    return pl.pallas_call(
        functools.partial(_ffn_kernel, tm=tm, chunk=FFN_ROW_CHUNK),
        grid=(b, s // tm),
        in_specs=[
            row,
            pl.BlockSpec((1, 3, d), lambda i, j: (i, 0, 0)),
            vec, vec,
            _resident((d, dff), lambda i, j: (0, 0)),
            _resident((d, dff), lambda i, j: (0, 0)),
            _resident((dff, d), lambda i, j: (0, 0)),
        ],
        out_specs=row,
        out_shape=jax.ShapeDtypeStruct((b, s, d), F32),
        compiler_params=_params(2),
        name="ffn",
    )(x, mod, gpre, gpost, wg, wu, wd)


def _token_mixer_kernel(x_ref, mod_ref, gpre_ref, gpost_ref, win_ref, lng_ref, ws_ref, bs_ref, wout_ref,
                        o_ref, *, d, tm, groups):
    x = x_ref[0]
    mod = mod_ref[0]
    h = _modulated_norm(x, gpre_ref[...], mod).astype(BF16)
    uv = jax.nn.gelu(_dot(h, win_ref[...]))
    u = uv[:, :d]
    v = uv[:, d:]
    vc = v - jnp.mean(v, axis=-1, keepdims=True)
    v = (vc * lax.rsqrt(jnp.mean(vc * vc, axis=-1, keepdims=True) + EPS) * lng_ref[...]).astype(BF16)

    pos_t = lax.broadcasted_iota(jnp.int32, (TM_CHUNK, TM_CHUNK), 0) // CHUNK
    pos_s = lax.broadcasted_iota(jnp.int32, (TM_CHUNK, TM_CHUNK), 1) // CHUNK
    chunk_causal = pos_t >= pos_s
    gd = d // groups
    bias = bs_ref[...]
    rows = []
    for n in range(tm // TM_CHUNK):
        cols = []
        for g in range(groups):
            w_g = jnp.where(chunk_causal, ws_ref[g], 0.0).astype(BF16)
            v_ng = v[n * TM_CHUNK:(n + 1) * TM_CHUNK, g * gd:(g + 1) * gd]
            cols.append(_dot(w_g, v_ng) + bias[:, g:g + 1])
        rows.append(jnp.concatenate(cols, axis=1))
    sv = jnp.concatenate(rows, axis=0) if len(rows) > 1 else rows[0]
    y = _dot((u * sv).astype(BF16), wout_ref[...])
    o_ref[0] = x + mod[2:3] * _rms(y, gpost_ref[...])


def _token_mixer(x, mod, gpre, gpost, w_in, ln_g, w_s, b_s_t, w_out, tm):
    b, s, d = x.shape
    groups, t, _ = w_s.shape
    row = pl.BlockSpec((1, tm, d), lambda i, j: (i, j, 0))
    vec = _resident((1, d), lambda i, j: (0, 0))
    return pl.pallas_call(
        functools.partial(_token_mixer_kernel, d=d, tm=tm, groups=groups),
        grid=(b, s // tm),
        in_specs=[
            row,
            pl.BlockSpec((1, 3, d), lambda i, j: (i, 0, 0)),
            vec, vec,
            _resident((d, 2 * d), lambda i, j: (0, 0)),
            vec,
            _resident((groups, t, t), lambda i, j: (0, 0, 0)),
            _resident((t, groups), lambda i, j: (0, 0)),
            _resident((d, d), lambda i, j: (0, 0)),
        ],
        out_specs=row,
        out_shape=jax.ShapeDtypeStruct((b, s, d), F32),
        compiler_params=_params(2),
        name="token_mixer",
    )(x, mod, gpre, gpost, w_in, ln_g, w_s, b_s_t, w_out)


def kernel(x, c, mix_ada_w, mix_ada_b, mix_pre_g, mix_post_g, sb_w_qkv, sb_w_o, tm_w_in, tm_ln_g, tm_w_s, tm_b_s, tm_w_out, ffn_ada_w, ffn_ada_b, ffn_pre_g, ffn_post_g, ffn_w_gate, ffn_w_up, ffn_w_down):
    b, s, d = x.shape
    depth = mix_ada_w.shape[0]
    assert depth == 2 and s % ATTN_TILE == 0 and d % LANES == 0

    mix_mod, ffn_mod = _ada(c, mix_ada_w, mix_ada_b, ffn_ada_w, ffn_ada_b)
    mix_mod = mix_mod.reshape(depth, b, 3, d)
    ffn_mod = ffn_mod.reshape(depth, b, 3, d)

    def vec(a, i):
        return a[i].reshape(1, d)

    def ffn_layer(x, i):
        return _ffn(x, ffn_mod[i], vec(ffn_pre_g, i), vec(ffn_post_g, i), ffn_w_gate[i].astype(BF16),
                    ffn_w_up[i].astype(BF16), ffn_w_down[i].astype(BF16), tm=512)

    qkv = _qkv(x, mix_mod[0], vec(mix_pre_g, 0), sb_w_qkv[0].astype(BF16), tm=512)
    attn = _sb_attn(qkv, d)
    x = _out_proj(attn, x, mix_mod[0], vec(mix_post_g, 0), sb_w_o[0].astype(BF16), tm=1024)
    x = ffn_layer(x, 0)

    x = _token_mixer(x, mix_mod[1], vec(mix_pre_g, 1), vec(mix_post_g, 1), tm_w_in[0].astype(BF16),
                     vec(tm_ln_g, 0), tm_w_s[0], tm_b_s[0].T, tm_w_out[0].astype(BF16), tm=512)
    x = ffn_layer(x, 1)
    return x
```

```python
import functools
import math

import jax
import jax.numpy as jnp
from jax import lax
from jax.experimental import pallas as pl
from jax.experimental.pallas import tpu as pltpu

EPS = 1e-6
CHUNK = 64
SB_HEAD_DIM = 64
TM_CHUNK = 128
LANES = 128
ATTN_TILE = 256
FFN_ROW_CHUNK = 128
NEAR_BLOCKS = 2
FAR_SKIP_LOG2 = 160.0
VMEM_LIMIT_BYTES = 56 * 1024 * 1024

F32 = jnp.float32
BF16 = jnp.bfloat16
NEG_BIG = -1e30


def _params(n_axes):
    return pltpu.CompilerParams(
        dimension_semantics=("parallel",) * n_axes,
        vmem_limit_bytes=VMEM_LIMIT_BYTES,
    )


def _resident(block_shape, index_map):
    return pl.BlockSpec(block_shape, index_map, pipeline_mode=pl.Buffered(1))


def _rms(x, g):
    return x * lax.rsqrt(jnp.mean(x * x, axis=-1, keepdims=True) + EPS) * g


def _modulated_norm(x, g, mod):
    return _rms(x, g) * (1.0 + mod[1:2]) + mod[0:1]


def _dot(a, b):
    return jnp.dot(a, b, preferred_element_type=F32)


def _ada_kernel(c_ref, mw_ref, mb_ref, fw_ref, fb_ref, mo_ref, fo_ref):
    s = jax.nn.silu(c_ref[...]).astype(BF16)
    mo_ref[0] = _dot(s, mw_ref[0].astype(BF16)) + mb_ref[0]
    fo_ref[0] = _dot(s, fw_ref[0].astype(BF16)) + fb_ref[0]


def _ada(c, mix_w, mix_b, ffn_w, ffn_b):
    depth, d, d3 = mix_w.shape
    b = c.shape[0]
    nblk = d3 // d
    w_spec = pl.BlockSpec((1, d, d), lambda l, n: (l, 0, n))
    b_spec = pl.BlockSpec((1, 1, d), lambda l, n: (l, 0, n))
    o_spec = pl.BlockSpec((1, b, d), lambda l, n: (l, 0, n))
    out = jax.ShapeDtypeStruct((depth, b, d3), F32)
    return pl.pallas_call(
        _ada_kernel,
        grid=(depth, nblk),
        in_specs=[pl.BlockSpec((b, d), lambda l, n: (0, 0)), w_spec, b_spec, w_spec, b_spec],
        out_specs=[o_spec, o_spec],
        out_shape=[out, out],
        compiler_params=_params(2),
        name="ada",
    )(c, mix_w, mix_b.reshape(depth, 1, d3), ffn_w, ffn_b.reshape(depth, 1, d3))


def _qkv_kernel(x_ref, mod_ref, g_ref, w_ref, o_ref, *, d, q_scale):
    h = _modulated_norm(x_ref[0], g_ref[...], mod_ref[0]).astype(BF16)
    acc = _dot(h, w_ref[...])
    o_ref[0, :, :d] = (acc[:, :d] * q_scale).astype(BF16)
    o_ref[0, :, d:] = acc[:, d:].astype(BF16)


def _qkv(x, mod, g, w, tm):
    b, s, d = x.shape
    n = w.shape[1]
    return pl.pallas_call(
        functools.partial(_qkv_kernel, d=d, q_scale=SB_HEAD_DIM ** -0.5 * math.log2(math.e)),
        grid=(b, s // tm),
        in_specs=[
            pl.BlockSpec((1, tm, d), lambda i, j: (i, j, 0)),
            pl.BlockSpec((1, 3, d), lambda i, j: (i, 0, 0)),
            _resident((1, d), lambda i, j: (0, 0)),
            _resident((d, n), lambda i, j: (0, 0)),
        ],
        out_specs=pl.BlockSpec((1, tm, n), lambda i, j: (i, j, 0)),
        out_shape=jax.ShapeDtypeStruct((b, s, n), BF16),
        compiler_params=_params(2),
        name="qkv",
    )(x, mod, g, w)


def _sb_tile(qh, k_t, v_t, tri, diag_mask, tile):
    z = lax.dot_general(qh, k_t, (((1,), (1,)), ((), ())), preferred_element_type=F32)
    if diag_mask is not None:
        z = jnp.where(diag_mask, z, NEG_BIG)
    neg_abs = lax.bitcast_convert_type(lax.bitcast_convert_type(z, jnp.uint32) | jnp.uint32(0x80000000), F32)
    sp = jnp.maximum(z, 0.0) + jnp.log2(1.0 + jnp.exp2(neg_abs))
    later = _dot(sp.astype(BF16), tri)
    a = jnp.exp2((z - sp) - later)
    o = _dot(a.astype(BF16), v_t)
    totals = [jnp.broadcast_to(later[r:r + tile, 0:1] + sp[r:r + tile, 0:1], (tile, LANES))
              for r in range(0, z.shape[0], tile)]
    return o, totals


def _head_lanes(rows, head):
    lane = lax.broadcasted_iota(jnp.int32, (rows, LANES), 1)
    return (lane < SB_HEAD_DIM) if head == 0 else (lane >= SB_HEAD_DIM)


def _sb_attn_kernel(q_ref, k_ref, v_ref, o_ref, acc_ref, carry_ref, *, seq, tile):
    nsub = seq // tile
    row = lax.broadcasted_iota(jnp.int32, (NEAR_BLOCKS * tile, tile), 0)
    col = lax.broadcasted_iota(jnp.int32, (NEAR_BLOCKS * tile, tile), 1)
    causal = (row >= tile) | (col < row)
    trow = lax.broadcasted_iota(jnp.int32, (tile, tile), 0)
    tcol = lax.broadcasted_iota(jnp.int32, (tile, tile), 1)
    tri = (trow > tcol).astype(BF16)
    causal_last = tcol < trow

    def q_rows(head, lo, hi):
        return jnp.where(_head_lanes(hi - lo, head), q_ref[0, lo:hi, :], 0)

    acc = [[None] * nsub for _ in range(2)]
    carry = [[None] * nsub for _ in range(2)]
    for j in range(nsub - 1, -1, -1):
        nb = min(NEAR_BLOCKS, nsub - j)
        lo, hi = j * tile, (j + nb) * tile
        k_t = k_ref[0, lo:lo + tile, :]
        v_t = v_ref[0, lo:lo + tile, :]
        mask = causal if nb == NEAR_BLOCKS else causal_last
        q_both = jnp.concatenate([q_rows(0, lo, hi), q_rows(1, lo, hi)], axis=0)
        o, totals = _sb_tile(q_both, k_t, v_t, tri, jnp.concatenate([mask, mask], axis=0), tile)
        for head in range(2):
            base = head * nb
            acc[head][j], carry[head][j] = o[base * tile:(base + 1) * tile], totals[base]
            for b in range(1, nb):
                r = j + b
                acc[head][r] = acc[head][r] + o[(base + b) * tile:(base + b + 1) * tile] * jnp.exp2(-carry[head][r])
                carry[head][r] = carry[head][r] + totals[base + b]
    block_low = []
    for head in range(2):
        for r in range(nsub):
            acc_ref[head, r * tile:(r + 1) * tile, :] = acc[head][r]
            carry_ref[head, r * tile:(r + 1) * tile, :] = carry[head][r]
        block_low.append([jnp.min(c) for c in carry[head]])

    for head in range(2):
        for r in range(NEAR_BLOCKS, nsub):
            rows = slice(r * tile, (r + 1) * tile)

            def keep_going(state):
                j, low = state
                return (j >= 0) & (low < FAR_SKIP_LOG2)

            def one_tile(state, head=head, r=r, rows=rows):
                j, _ = state
                keys = pl.ds(pl.multiple_of(j * tile, tile), tile)
                o, totals = _sb_tile(q_rows(head, r * tile, (r + 1) * tile), k_ref[0, keys, :], v_ref[0, keys, :],
                                     tri, None, tile)
                c = carry_ref[head, rows, :]
                acc_ref[head, rows, :] = acc_ref[head, rows, :] + o * jnp.exp2(-c)
                c = c + totals[0]
                carry_ref[head, rows, :] = c
                return j - 1, jnp.min(c)

            lax.while_loop(keep_going, one_tile, (jnp.int32(r - NEAR_BLOCKS), block_low[head][r]))

    o_ref[0] = jnp.where(_head_lanes(seq, 0), acc_ref[0], acc_ref[1]).astype(BF16)


def _sb_attn(qkv, d):
    b, s, _ = qkv.shape
    pairs = d // LANES

    def spec(offset):
        return pl.BlockSpec((1, s, LANES), lambda i, p: (i, 0, p + offset))

    return pl.pallas_call(
        functools.partial(_sb_attn_kernel, seq=s, tile=ATTN_TILE),
        grid=(b, pairs),
        in_specs=[spec(0), spec(pairs), spec(2 * pairs)],
        out_specs=pl.BlockSpec((1, s, LANES), lambda i, p: (i, 0, p)),
        out_shape=jax.ShapeDtypeStruct((b, s, d), BF16),
        scratch_shapes=[pltpu.VMEM((2, s, LANES), F32), pltpu.VMEM((2, s, LANES), F32)],
        compiler_params=_params(2),
        name="sb_attn",
    )(qkv, qkv, qkv)


def _out_proj_kernel(a_ref, x_ref, mod_ref, g_ref, w_ref, o_ref):
    y = _dot(a_ref[0], w_ref[...])
    o_ref[0] = x_ref[0] + mod_ref[0][2:3] * _rms(y, g_ref[...])


def _out_proj(a, x, mod, g, w, tm):
    b, s, d = x.shape
    row = pl.BlockSpec((1, tm, d), lambda i, j: (i, j, 0))
    return pl.pallas_call(
        _out_proj_kernel,
        grid=(b, s // tm),
        in_specs=[
            row, row,
            pl.BlockSpec((1, 3, d), lambda i, j: (i, 0, 0)),
            _resident((1, d), lambda i, j: (0, 0)),
            _resident((d, d), lambda i, j: (0, 0)),
        ],
        out_specs=row,
        out_shape=jax.ShapeDtypeStruct((b, s, d), F32),
        compiler_params=_params(2),
        name="out_proj",
    )(a, x, mod, g, w)


def _ffn_kernel(x_ref, mod_ref, gpre_ref, gpost_ref, wg_ref, wu_ref, wd_ref, o_ref, *, tm, chunk):
    mod = mod_ref[0]
    for r in range(0, tm, chunk):
        x = x_ref[0, r:r + chunk]
        h = _modulated_norm(x, gpre_ref[...], mod).astype(BF16)
        act = (jax.nn.silu(_dot(h, wg_ref[...])) * _dot(h, wu_ref[...])).astype(BF16)
        y = _dot(act, wd_ref[...])
        o_ref[0, r:r + chunk] = x + mod[2:3] * _rms(y, gpost_ref[...])


def _ffn(x, mod, gpre, gpost, wg, wu, wd, tm):
    b, s, d = x.shape
    dff = wg.shape[1]
    row = pl.BlockSpec((1, tm, d), lambda i, j: (i, j, 0))
    vec = _resident((1, d), lambda i, j: (0, 0))
    return pl.pallas_call(
        functools.partial(_ffn_kernel, tm=tm, chunk=FFN_ROW_CHUNK),
        grid=(b, s // tm),
        in_specs=[
            row,
            pl.BlockSpec((1, 3, d), lambda i, j: (i, 0, 0)),
            vec, vec,
            _resident((d, dff), lambda i, j: (0, 0)),
            _resident((d, dff), lambda i, j: (0, 0)),
            _resident((dff, d), lambda i, j: (0, 0)),
        ],
        out_specs=row,
        out_shape=jax.ShapeDtypeStruct((b, s, d), F32),
        compiler_params=_params(2),
        name="ffn",
    )(x, mod, gpre, gpost, wg, wu, wd)


def _token_mixer_kernel(x_ref, mod_ref, gpre_ref, gpost_ref, win_ref, lng_ref, ws_ref, bs_ref, wout_ref,
                        o_ref, *, d, tm, groups):
    x = x_ref[0]
    mod = mod_ref[0]
    h = _modulated_norm(x, gpre_ref[...], mod).astype(BF16)
    uv = jax.nn.gelu(_dot(h, win_ref[...]))
    u = uv[:, :d]
    v = uv[:, d:]
    vc = v - jnp.mean(v, axis=-1, keepdims=True)
    v = (vc * lax.rsqrt(jnp.mean(vc * vc, axis=-1, keepdims=True) + EPS) * lng_ref[...]).astype(BF16)

    pos_t = lax.broadcasted_iota(jnp.int32, (TM_CHUNK, TM_CHUNK), 0) // CHUNK
    pos_s = lax.broadcasted_iota(jnp.int32, (TM_CHUNK, TM_CHUNK), 1) // CHUNK
    chunk_causal = pos_t >= pos_s
    gd = d // groups
    bias = bs_ref[...]
    rows = []
    for n in range(tm // TM_CHUNK):
        cols = []
        for g in range(groups):
            w_g = jnp.where(chunk_causal, ws_ref[g], 0.0).astype(BF16)
            v_ng = v[n * TM_CHUNK:(n + 1) * TM_CHUNK, g * gd:(g + 1) * gd]
            cols.append(_dot(w_g, v_ng) + bias[:, g:g + 1])
        rows.append(jnp.concatenate(cols, axis=1))
    sv = jnp.concatenate(rows, axis=0) if len(rows) > 1 else rows[0]
    y = _dot((u * sv).astype(BF16), wout_ref[...])
    o_ref[0] = x + mod[2:3] * _rms(y, gpost_ref[...])


def _token_mixer(x, mod, gpre, gpost, w_in, ln_g, w_s, b_s_t, w_out, tm):
    b, s, d = x.shape
    groups, t, _ = w_s.shape
    row = pl.BlockSpec((1, tm, d), lambda i, j: (i, j, 0))
    vec = _resident((1, d), lambda i, j: (0, 0))
    return pl.pallas_call(
        functools.partial(_token_mixer_kernel, d=d, tm=tm, groups=groups),
        grid=(b, s // tm),
        in_specs=[
            row,
            pl.BlockSpec((1, 3, d), lambda i, j: (i, 0, 0)),
            vec, vec,
            _resident((d, 2 * d), lambda i, j: (0, 0)),
            vec,
            _resident((groups, t, t), lambda i, j: (0, 0, 0)),
            _resident((t, groups), lambda i, j: (0, 0)),
            _resident((d, d), lambda i, j: (0, 0)),
        ],
        out_specs=row,
        out_shape=jax.ShapeDtypeStruct((b, s, d), F32),
        compiler_params=_params(2),
        name="token_mixer",
    )(x, mod, gpre, gpost, w_in, ln_g, w_s, b_s_t, w_out)


def kernel(x, c, mix_ada_w, mix_ada_b, mix_pre_g, mix_post_g, sb_w_qkv, sb_w_o, tm_w_in, tm_ln_g, tm_w_s, tm_b_s, tm_w_out, ffn_ada_w, ffn_ada_b, ffn_pre_g, ffn_post_g, ffn_w_gate, ffn_w_up, ffn_w_down):
    b, s, d = x.shape
    depth = mix_ada_w.shape[0]
    assert depth == 2 and s % ATTN_TILE == 0 and d % LANES == 0

    mix_mod, ffn_mod = _ada(c, mix_ada_w, mix_ada_b, ffn_ada_w, ffn_ada_b)
    mix_mod = mix_mod.reshape(depth, b, 3, d)
    ffn_mod = ffn_mod.reshape(depth, b, 3, d)

    def vec(a, i):
        return a[i].reshape(1, d)

    def ffn_layer(x, i):
        return _ffn(x, ffn_mod[i], vec(ffn_pre_g, i), vec(ffn_post_g, i), ffn_w_gate[i].astype(BF16),
                    ffn_w_up[i].astype(BF16), ffn_w_down[i].astype(BF16), tm=512)

    qkv = _qkv(x, mix_mod[0], vec(mix_pre_g, 0), sb_w_qkv[0].astype(BF16), tm=512)
    attn = _sb_attn(qkv, d)
    x = _out_proj(attn, x, mix_mod[0], vec(mix_post_g, 0), sb_w_o[0].astype(BF16), tm=1024)
    x = ffn_layer(x, 0)

    x = _token_mixer(x, mix_mod[1], vec(mix_pre_g, 1), vec(mix_post_g, 1), tm_w_in[0].astype(BF16),
                     vec(tm_ln_g, 0), tm_w_s[0], tm_b_s[0].T, tm_w_out[0].astype(BF16), tm=512)
    x = ffn_layer(x, 1)
    return x
```

```python
import functools
import math

import jax
import jax.numpy as jnp
from jax import lax
from jax.experimental import pallas as pl
from jax.experimental.pallas import tpu as pltpu

EPS = 1e-6
CHUNK = 64
SB_HEAD_DIM = 64
TM_CHUNK = 128
LANES = 128
ATTN_TILE = 256
FFN_ROW_CHUNK = 256
MIXER_ROW_CHUNK = 256
NEAR_BLOCKS = 2
FAR_SKIP_LOG2 = 160.0
VMEM_LIMIT_BYTES = 56 * 1024 * 1024

F32 = jnp.float32
BF16 = jnp.bfloat16
NEG_BIG = -1e30


def _params(n_axes):
    return pltpu.CompilerParams(
        dimension_semantics=("parallel",) * n_axes,
        vmem_limit_bytes=VMEM_LIMIT_BYTES,
    )


def _resident(block_shape, index_map):
    return pl.BlockSpec(block_shape, index_map, pipeline_mode=pl.Buffered(1))


def _rms(x, g):
    return x * lax.rsqrt(jnp.mean(x * x, axis=-1, keepdims=True) + EPS) * g


def _modulated_norm(x, g, mod):
    return _rms(x, g) * (1.0 + mod[1:2]) + mod[0:1]


def _dot(a, b):
    return jnp.dot(a, b, preferred_element_type=F32)


def _ada_kernel(c_ref, mw_ref, mb_ref, fw_ref, fb_ref, mo_ref, fo_ref):
    s = jax.nn.silu(c_ref[...]).astype(BF16)
    mo_ref[0] = _dot(s, mw_ref[0].astype(BF16)) + mb_ref[0]
    fo_ref[0] = _dot(s, fw_ref[0].astype(BF16)) + fb_ref[0]


def _ada(c, mix_w, mix_b, ffn_w, ffn_b):
    depth, d, d3 = mix_w.shape
    b = c.shape[0]
    nblk = d3 // d
    w_spec = pl.BlockSpec((1, d, d), lambda l, n: (l, 0, n))
    b_spec = pl.BlockSpec((1, 1, d), lambda l, n: (l, 0, n))
    o_spec = pl.BlockSpec((1, b, d), lambda l, n: (l, 0, n))
    out = jax.ShapeDtypeStruct((depth, b, d3), F32)
    return pl.pallas_call(
        _ada_kernel,
        grid=(depth, nblk),
        in_specs=[pl.BlockSpec((b, d), lambda l, n: (0, 0)), w_spec, b_spec, w_spec, b_spec],
        out_specs=[o_spec, o_spec],
        out_shape=[out, out],
        compiler_params=_params(2),
        name="ada",
    )(c, mix_w, mix_b.reshape(depth, 1, d3), ffn_w, ffn_b.reshape(depth, 1, d3))


def _qkv_kernel(x_ref, mod_ref, g_ref, w_ref, o_ref, *, d, q_scale):
    h = _modulated_norm(x_ref[0], g_ref[...], mod_ref[0]).astype(BF16)
    acc = _dot(h, w_ref[...])
    o_ref[0, :, :d] = (acc[:, :d] * q_scale).astype(BF16)
    o_ref[0, :, d:] = acc[:, d:].astype(BF16)


def _qkv(x, mod, g, w, tm):
    b, s, d = x.shape
    n = w.shape[1]
    return pl.pallas_call(
        functools.partial(_qkv_kernel, d=d, q_scale=SB_HEAD_DIM ** -0.5 * math.log2(math.e)),
        grid=(b, s // tm),
        in_specs=[
            pl.BlockSpec((1, tm, d), lambda i, j: (i, j, 0)),
            pl.BlockSpec((1, 3, d), lambda i, j: (i, 0, 0)),
            _resident((1, d), lambda i, j: (0, 0)),
            _resident((d, n), lambda i, j: (0, 0)),
        ],
        out_specs=pl.BlockSpec((1, tm, n), lambda i, j: (i, j, 0)),
        out_shape=jax.ShapeDtypeStruct((b, s, n), BF16),
        compiler_params=_params(2),
        name="qkv",
    )(x, mod, g, w)


def _sb_tile(qh, k_t, v_t, tri, diag_mask, tile):
    z = lax.dot_general(qh, k_t, (((1,), (1,)), ((), ())), preferred_element_type=F32)
    if diag_mask is not None:
        z = jnp.where(diag_mask, z, NEG_BIG)
    neg_abs = lax.bitcast_convert_type(lax.bitcast_convert_type(z, jnp.uint32) | jnp.uint32(0x80000000), F32)
    sp = jnp.maximum(z, 0.0) + jnp.log2(1.0 + jnp.exp2(neg_abs))
    later = _dot(sp.astype(BF16), tri)
    a = jnp.exp2((z - sp) - later)
    o = _dot(a.astype(BF16), v_t)
    totals = [jnp.broadcast_to(later[r:r + tile, 0:1] + sp[r:r + tile, 0:1], (tile, LANES))
              for r in range(0, z.shape[0], tile)]
    return o, totals


def _head_lanes(rows, head):
    lane = lax.broadcasted_iota(jnp.int32, (rows, LANES), 1)
    return (lane < SB_HEAD_DIM) if head == 0 else (lane >= SB_HEAD_DIM)


def _sb_attn_kernel(q_ref, k_ref, v_ref, o_ref, acc_ref, carry_ref, *, seq, tile):
    nsub = seq // tile
    row = lax.broadcasted_iota(jnp.int32, (NEAR_BLOCKS * tile, tile), 0)
    col = lax.broadcasted_iota(jnp.int32, (NEAR_BLOCKS * tile, tile), 1)
    causal = (row >= tile) | (col < row)
    trow = lax.broadcasted_iota(jnp.int32, (tile, tile), 0)
    tcol = lax.broadcasted_iota(jnp.int32, (tile, tile), 1)
    tri = (trow > tcol).astype(BF16)
    causal_last = tcol < trow

    def q_rows(head, lo, hi):
        return jnp.where(_head_lanes(hi - lo, head), q_ref[0, lo:hi, :], 0)

    acc = [[None] * nsub for _ in range(2)]
    carry = [[None] * nsub for _ in range(2)]
    for j in range(nsub - 1, -1, -1):
        nb = min(NEAR_BLOCKS, nsub - j)
        lo, hi = j * tile, (j + nb) * tile
        k_t = k_ref[0, lo:lo + tile, :]
        v_t = v_ref[0, lo:lo + tile, :]
        mask = causal if nb == NEAR_BLOCKS else causal_last
        q_both = jnp.concatenate([q_rows(0, lo, hi), q_rows(1, lo, hi)], axis=0)
        o, totals = _sb_tile(q_both, k_t, v_t, tri, jnp.concatenate([mask, mask], axis=0), tile)
        for head in range(2):
            base = head * nb
            acc[head][j], carry[head][j] = o[base * tile:(base + 1) * tile], totals[base]
            for b in range(1, nb):
                r = j + b
                acc[head][r] = acc[head][r] + o[(base + b) * tile:(base + b + 1) * tile] * jnp.exp2(-carry[head][r])
                carry[head][r] = carry[head][r] + totals[base + b]
    block_low = []
    for head in range(2):
        for r in range(nsub):
            acc_ref[head, r * tile:(r + 1) * tile, :] = acc[head][r]
            carry_ref[head, r * tile:(r + 1) * tile, :] = carry[head][r]
        block_low.append([jnp.min(c) for c in carry[head]])

    for head in range(2):
        for r in range(NEAR_BLOCKS, nsub):
            rows = slice(r * tile, (r + 1) * tile)

            def keep_going(state):
                j, low = state
                return (j >= 0) & (low < FAR_SKIP_LOG2)

            def one_tile(state, head=head, r=r, rows=rows):
                j, _ = state
                keys = pl.ds(pl.multiple_of(j * tile, tile), tile)
                o, totals = _sb_tile(q_rows(head, r * tile, (r + 1) * tile), k_ref[0, keys, :], v_ref[0, keys, :],
                                     tri, None, tile)
                c = carry_ref[head, rows, :]
                acc_ref[head, rows, :] = acc_ref[head, rows, :] + o * jnp.exp2(-c)
                c = c + totals[0]
                carry_ref[head, rows, :] = c
                return j - 1, jnp.min(c)

            lax.while_loop(keep_going, one_tile, (jnp.int32(r - NEAR_BLOCKS), block_low[head][r]))

    o_ref[0] = jnp.where(_head_lanes(seq, 0), acc_ref[0], acc_ref[1]).astype(BF16)


def _sb_attn(qkv, d):
    b, s, _ = qkv.shape
    pairs = d // LANES

    def spec(offset):
        return pl.BlockSpec((1, s, LANES), lambda i, p: (i, 0, p + offset))

    return pl.pallas_call(
        functools.partial(_sb_attn_kernel, seq=s, tile=ATTN_TILE),
        grid=(b, pairs),
        in_specs=[spec(0), spec(pairs), spec(2 * pairs)],
        out_specs=pl.BlockSpec((1, s, LANES), lambda i, p: (i, 0, p)),
        out_shape=jax.ShapeDtypeStruct((b, s, d), BF16),
        scratch_shapes=[pltpu.VMEM((2, s, LANES), F32), pltpu.VMEM((2, s, LANES), F32)],
        compiler_params=_params(2),
        name="sb_attn",
    )(qkv, qkv, qkv)


def _out_proj_kernel(a_ref, x_ref, mod_ref, g_ref, w_ref, o_ref):
    y = _dot(a_ref[0], w_ref[...])
    o_ref[0] = x_ref[0] + mod_ref[0][2:3] * _rms(y, g_ref[...])


def _out_proj(a, x, mod, g, w, tm):
    b, s, d = x.shape
    row = pl.BlockSpec((1, tm, d), lambda i, j: (i, j, 0))
    return pl.pallas_call(
        _out_proj_kernel,
        grid=(b, s // tm),
        in_specs=[
            row, row,
            pl.BlockSpec((1, 3, d), lambda i, j: (i, 0, 0)),
            _resident((1, d), lambda i, j: (0, 0)),
            _resident((d, d), lambda i, j: (0, 0)),
        ],
        out_specs=row,
        out_shape=jax.ShapeDtypeStruct((b, s, d), F32),
        compiler_params=_params(2),
        name="out_proj",
    )(a, x, mod, g, w)


def _ffn_kernel(x_ref, mod_ref, gpre_ref, gpost_ref, wg_ref, wu_ref, wd_ref, o_ref, *, tm, chunk):
    mod = mod_ref[0]
    for r in range(0, tm, chunk):
        x = x_ref[0, r:r + chunk]
        h = _modulated_norm(x, gpre_ref[...], mod).astype(BF16)
        act = (jax.nn.silu(_dot(h, wg_ref[...])) * _dot(h, wu_ref[...])).astype(BF16)
        y = _dot(act, wd_ref[...])
        o_ref[0, r:r + chunk] = x + mod[2:3] * _rms(y, gpost_ref[...])


def _ffn(x, mod, gpre, gpost, wg, wu, wd, tm):
    b, s, d = x.shape
    dff = wg.shape[1]
    row = pl.BlockSpec((1, tm, d), lambda i, j: (i, j, 0))
    vec = _resident((1, d), lambda i, j: (0, 0))
    return pl.pallas_call(
        functools.partial(_ffn_kernel, tm=tm, chunk=FFN_ROW_CHUNK),
        grid=(b, s // tm),
        in_specs=[
            row,
            pl.BlockSpec((1, 3, d), lambda i, j: (i, 0, 0)),
            vec, vec,
            _resident((d, dff), lambda i, j: (0, 0)),
            _resident((d, dff), lambda i, j: (0, 0)),
            _resident((dff, d), lambda i, j: (0, 0)),
        ],
        out_specs=row,
        out_shape=jax.ShapeDtypeStruct((b, s, d), F32),
        compiler_params=_params(2),
        name="ffn",
    )(x, mod, gpre, gpost, wg, wu, wd)


def _token_mixer_kernel(x_ref, mod_ref, gpre_ref, gpost_ref, win_ref, lng_ref, ws_ref, bs_ref, wout_ref,
                        o_ref, *, d, tm, groups, chunk):
    mod = mod_ref[0]
    pos_t = lax.broadcasted_iota(jnp.int32, (TM_CHUNK, TM_CHUNK), 0) // CHUNK
    pos_s = lax.broadcasted_iota(jnp.int32, (TM_CHUNK, TM_CHUNK), 1) // CHUNK
    chunk_causal = pos_t >= pos_s
    gd = d // groups
    bias = bs_ref[...]
    w_s = [jnp.where(chunk_causal, ws_ref[g], 0.0).astype(BF16) for g in range(groups)]
    starts = list(range(0, tm, chunk))
    xs = [x_ref[0, r:r + chunk] for r in starts]
    uvs = [_dot(_modulated_norm(x, gpre_ref[...], mod).astype(BF16), win_ref[...]) for x in xs]
    ys = []
    for uv in uvs:
        uv = jax.nn.gelu(uv)
        u = uv[:, :d]
        v = uv[:, d:]
        vc = v - jnp.mean(v, axis=-1, keepdims=True)
        v = (vc * lax.rsqrt(jnp.mean(vc * vc, axis=-1, keepdims=True) + EPS) * lng_ref[...]).astype(BF16)
        rows = []
        for n in range(chunk // TM_CHUNK):
            cols = []
            for g in range(groups):
                v_ng = v[n * TM_CHUNK:(n + 1) * TM_CHUNK, g * gd:(g + 1) * gd]
                cols.append(_dot(w_s[g], v_ng) + bias[:, g:g + 1])
            rows.append(jnp.concatenate(cols, axis=1))
        sv = jnp.concatenate(rows, axis=0) if len(rows) > 1 else rows[0]
        ys.append((u * sv).astype(BF16))
    for r, x, y in zip(starts, xs, ys):
        o_ref[0, r:r + chunk] = x + mod[2:3] * _rms(_dot(y, wout_ref[...]), gpost_ref[...])


def _token_mixer(x, mod, gpre, gpost, w_in, ln_g, w_s, b_s_t, w_out, tm):
    b, s, d = x.shape
    groups, t, _ = w_s.shape
    row = pl.BlockSpec((1, tm, d), lambda i, j: (i, j, 0))
    vec = _resident((1, d), lambda i, j: (0, 0))
    return pl.pallas_call(
        functools.partial(_token_mixer_kernel, d=d, tm=tm, groups=groups, chunk=MIXER_ROW_CHUNK),
        grid=(b, s // tm),
        in_specs=[
            row,
            pl.BlockSpec((1, 3, d), lambda i, j: (i, 0, 0)),
            vec, vec,
            _resident((d, 2 * d), lambda i, j: (0, 0)),
            vec,
            _resident((groups, t, t), lambda i, j: (0, 0, 0)),
            _resident((t, groups), lambda i, j: (0, 0)),
            _resident((d, d), lambda i, j: (0, 0)),
        ],
        out_specs=row,
        out_shape=jax.ShapeDtypeStruct((b, s, d), F32),
        compiler_params=_params(2),
        name="token_mixer",
    )(x, mod, gpre, gpost, w_in, ln_g, w_s, b_s_t, w_out)


def kernel(x, c, mix_ada_w, mix_ada_b, mix_pre_g, mix_post_g, sb_w_qkv, sb_w_o, tm_w_in, tm_ln_g, tm_w_s, tm_b_s, tm_w_out, ffn_ada_w, ffn_ada_b, ffn_pre_g, ffn_post_g, ffn_w_gate, ffn_w_up, ffn_w_down):
    b, s, d = x.shape
    depth = mix_ada_w.shape[0]
    assert depth == 2 and s % ATTN_TILE == 0 and d % LANES == 0

    mix_mod, ffn_mod = _ada(c, mix_ada_w, mix_ada_b, ffn_ada_w, ffn_ada_b)
    mix_mod = mix_mod.reshape(depth, b, 3, d)
    ffn_mod = ffn_mod.reshape(depth, b, 3, d)

    def vec(a, i):
        return a[i].reshape(1, d)

    def ffn_layer(x, i):
        return _ffn(x, ffn_mod[i], vec(ffn_pre_g, i), vec(ffn_post_g, i), ffn_w_gate[i].astype(BF16),
                    ffn_w_up[i].astype(BF16), ffn_w_down[i].astype(BF16), tm=1024)

    qkv = _qkv(x, mix_mod[0], vec(mix_pre_g, 0), sb_w_qkv[0].astype(BF16), tm=512)
    attn = _sb_attn(qkv, d)
    x = _out_proj(attn, x, mix_mod[0], vec(mix_post_g, 0), sb_w_o[0].astype(BF16), tm=1024)
    x = ffn_layer(x, 0)

    x = _token_mixer(x, mix_mod[1], vec(mix_pre_g, 1), vec(mix_post_g, 1), tm_w_in[0].astype(BF16),
                     vec(tm_ln_g, 0), tm_w_s[0], tm_b_s[0].T, tm_w_out[0].astype(BF16), tm=512)
    x = ffn_layer(x, 1)
    return x
```

```python
import functools
import math

import jax
import jax.numpy as jnp
from jax import lax
from jax.experimental import pallas as pl
from jax.experimental.pallas import tpu as pltpu

EPS = 1e-6
CHUNK = 64
SB_HEAD_DIM = 64
TM_CHUNK = 128
LANES = 128
ATTN_TILE = 256
FFN_ROW_CHUNK = 256
MIXER_ROW_CHUNK = 256
NEAR_BLOCKS = 2
FAR_SKIP_LOG2 = 160.0
VMEM_LIMIT_BYTES = 56 * 1024 * 1024

F32 = jnp.float32
BF16 = jnp.bfloat16
NEG_BIG = -1e30


def _params(n_axes):
    return pltpu.CompilerParams(
        dimension_semantics=("parallel",) * n_axes,
        vmem_limit_bytes=VMEM_LIMIT_BYTES,
    )


def _resident(block_shape, index_map):
    return pl.BlockSpec(block_shape, index_map, pipeline_mode=pl.Buffered(1))


def _rms(x, g):
    return x * lax.rsqrt(jnp.mean(x * x, axis=-1, keepdims=True) + EPS) * g


def _modulated_norm(x, g, mod):
    return _rms(x, g) * (1.0 + mod[1:2]) + mod[0:1]


def _dot(a, b):
    return jnp.dot(a, b, preferred_element_type=F32)


def _ada_kernel(c_ref, mw_ref, mb_ref, fw_ref, fb_ref, mo_ref, fo_ref):
    s = jax.nn.silu(c_ref[...]).astype(BF16)
    mo_ref[0] = _dot(s, mw_ref[0].astype(BF16)) + mb_ref[0]
    fo_ref[0] = _dot(s, fw_ref[0].astype(BF16)) + fb_ref[0]


def _ada(c, mix_w, mix_b, ffn_w, ffn_b):
    depth, d, d3 = mix_w.shape
    b = c.shape[0]
    nblk = d3 // d
    w_spec = pl.BlockSpec((1, d, d), lambda l, n: (l, 0, n))
    b_spec = pl.BlockSpec((1, 1, d), lambda l, n: (l, 0, n))
    o_spec = pl.BlockSpec((1, b, d), lambda l, n: (l, 0, n))
    out = jax.ShapeDtypeStruct((depth, b, d3), F32)
    return pl.pallas_call(
        _ada_kernel,
        grid=(depth, nblk),
        in_specs=[pl.BlockSpec((b, d), lambda l, n: (0, 0)), w_spec, b_spec, w_spec, b_spec],
        out_specs=[o_spec, o_spec],
        out_shape=[out, out],
        compiler_params=_params(2),
        name="ada",
    )(c, mix_w, mix_b.reshape(depth, 1, d3), ffn_w, ffn_b.reshape(depth, 1, d3))


def _qkv_kernel(x_ref, mod_ref, g_ref, w_ref, o_ref, *, d, q_scale, tm, chunk):
    mod = mod_ref[0]
    for r in range(0, tm, chunk):
        h = _modulated_norm(x_ref[0, r:r + chunk], g_ref[...], mod).astype(BF16)
        acc = _dot(h, w_ref[...])
        o_ref[0, r:r + chunk, :d] = (acc[:, :d] * q_scale).astype(BF16)
        o_ref[0, r:r + chunk, d:] = acc[:, d:].astype(BF16)


def _qkv(x, mod, g, w, tm):
    b, s, d = x.shape
    n = w.shape[1]
    return pl.pallas_call(
        functools.partial(_qkv_kernel, d=d, q_scale=SB_HEAD_DIM ** -0.5 * math.log2(math.e), tm=tm,
                          chunk=FFN_ROW_CHUNK),
        grid=(b, s // tm),
        in_specs=[
            pl.BlockSpec((1, tm, d), lambda i, j: (i, j, 0)),
            pl.BlockSpec((1, 3, d), lambda i, j: (i, 0, 0)),
            _resident((1, d), lambda i, j: (0, 0)),
            _resident((d, n), lambda i, j: (0, 0)),
        ],
        out_specs=pl.BlockSpec((1, tm, n), lambda i, j: (i, j, 0)),
        out_shape=jax.ShapeDtypeStruct((b, s, n), BF16),
        compiler_params=_params(2),
        name="qkv",
    )(x, mod, g, w)


def _sb_tile(qh, k_t, v_t, tri, diag_mask, tile):
    z = lax.dot_general(qh, k_t, (((1,), (1,)), ((), ())), preferred_element_type=F32)
    if diag_mask is not None:
        z = jnp.where(diag_mask, z, NEG_BIG)
    neg_abs = lax.bitcast_convert_type(lax.bitcast_convert_type(z, jnp.uint32) | jnp.uint32(0x80000000), F32)
    sp = jnp.maximum(z, 0.0) + jnp.log2(1.0 + jnp.exp2(neg_abs))
    later = _dot(sp.astype(BF16), tri)
    a = jnp.exp2((z - sp) - later)
    o = _dot(a.astype(BF16), v_t)
    totals = [jnp.broadcast_to(later[r:r + tile, 0:1] + sp[r:r + tile, 0:1], (tile, LANES))
              for r in range(0, z.shape[0], tile)]
    return o, totals


def _head_lanes(rows, head):
    lane = lax.broadcasted_iota(jnp.int32, (rows, LANES), 1)
    return (lane < SB_HEAD_DIM) if head == 0 else (lane >= SB_HEAD_DIM)


def _sb_attn_kernel(q_ref, k_ref, v_ref, o_ref, acc_ref, carry_ref, *, seq, tile):
    nsub = seq // tile
    row = lax.broadcasted_iota(jnp.int32, (NEAR_BLOCKS * tile, tile), 0)
    col = lax.broadcasted_iota(jnp.int32, (NEAR_BLOCKS * tile, tile), 1)
    causal = (row >= tile) | (col < row)
    trow = lax.broadcasted_iota(jnp.int32, (tile, tile), 0)
    tcol = lax.broadcasted_iota(jnp.int32, (tile, tile), 1)
    tri = (trow > tcol).astype(BF16)
    causal_last = tcol < trow

    def q_rows(head, lo, hi):
        return jnp.where(_head_lanes(hi - lo, head), q_ref[0, lo:hi, :], 0)

    acc = [[None] * nsub for _ in range(2)]
    carry = [[None] * nsub for _ in range(2)]
    for j in range(nsub - 1, -1, -1):
        nb = min(NEAR_BLOCKS, nsub - j)
        lo, hi = j * tile, (j + nb) * tile
        k_t = k_ref[0, lo:lo + tile, :]
        v_t = v_ref[0, lo:lo + tile, :]
        mask = causal if nb == NEAR_BLOCKS else causal_last
        q_both = jnp.concatenate([q_rows(0, lo, hi), q_rows(1, lo, hi)], axis=0)
        o, totals = _sb_tile(q_both, k_t, v_t, tri, jnp.concatenate([mask, mask], axis=0), tile)
        for head in range(2):
            base = head * nb
            acc[head][j], carry[head][j] = o[base * tile:(base + 1) * tile], totals[base]
            for b in range(1, nb):
                r = j + b
                acc[head][r] = acc[head][r] + o[(base + b) * tile:(base + b + 1) * tile] * jnp.exp2(-carry[head][r])
                carry[head][r] = carry[head][r] + totals[base + b]
    block_low = []
    for head in range(2):
        for r in range(nsub):
            acc_ref[head, r * tile:(r + 1) * tile, :] = acc[head][r]
            carry_ref[head, r * tile:(r + 1) * tile, :] = carry[head][r]
        block_low.append([jnp.min(c) for c in carry[head]])

    for head in range(2):
        for r in range(NEAR_BLOCKS, nsub):
            rows = slice(r * tile, (r + 1) * tile)

            def keep_going(state):
                j, low = state
                return (j >= 0) & (low < FAR_SKIP_LOG2)

            def one_tile(state, head=head, r=r, rows=rows):
                j, _ = state
                keys = pl.ds(pl.multiple_of(j * tile, tile), tile)
                o, totals = _sb_tile(q_rows(head, r * tile, (r + 1) * tile), k_ref[0, keys, :], v_ref[0, keys, :],
                                     tri, None, tile)
                c = carry_ref[head, rows, :]
                acc_ref[head, rows, :] = acc_ref[head, rows, :] + o * jnp.exp2(-c)
                c = c + totals[0]
                carry_ref[head, rows, :] = c
                return j - 1, jnp.min(c)

            lax.while_loop(keep_going, one_tile, (jnp.int32(r - NEAR_BLOCKS), block_low[head][r]))

    o_ref[0] = jnp.where(_head_lanes(seq, 0), acc_ref[0], acc_ref[1]).astype(BF16)


def _sb_attn(qkv, d):
    b, s, _ = qkv.shape
    pairs = d // LANES

    def spec(offset):
        return pl.BlockSpec((1, s, LANES), lambda i, p: (i, 0, p + offset))

    return pl.pallas_call(
        functools.partial(_sb_attn_kernel, seq=s, tile=ATTN_TILE),
        grid=(b, pairs),
        in_specs=[spec(0), spec(pairs), spec(2 * pairs)],
        out_specs=pl.BlockSpec((1, s, LANES), lambda i, p: (i, 0, p)),
        out_shape=jax.ShapeDtypeStruct((b, s, d), BF16),
        scratch_shapes=[pltpu.VMEM((2, s, LANES), F32), pltpu.VMEM((2, s, LANES), F32)],
        compiler_params=_params(2),
        name="sb_attn",
    )(qkv, qkv, qkv)


def _out_proj_kernel(a_ref, x_ref, mod_ref, g_ref, w_ref, o_ref):
    y = _dot(a_ref[0], w_ref[...])
    o_ref[0] = x_ref[0] + mod_ref[0][2:3] * _rms(y, g_ref[...])


def _out_proj(a, x, mod, g, w, tm):
    b, s, d = x.shape
    row = pl.BlockSpec((1, tm, d), lambda i, j: (i, j, 0))
    return pl.pallas_call(
        _out_proj_kernel,
        grid=(b, s // tm),
        in_specs=[
            row, row,
            pl.BlockSpec((1, 3, d), lambda i, j: (i, 0, 0)),
            _resident((1, d), lambda i, j: (0, 0)),
            _resident((d, d), lambda i, j: (0, 0)),
        ],
        out_specs=row,
        out_shape=jax.ShapeDtypeStruct((b, s, d), F32),
        compiler_params=_params(2),
        name="out_proj",
    )(a, x, mod, g, w)


def _ffn_kernel(x_ref, mod_ref, gpre_ref, gpost_ref, wg_ref, wu_ref, wd_ref, o_ref, *, tm, chunk):
    mod = mod_ref[0]
    for r in range(0, tm, chunk):
        x = x_ref[0, r:r + chunk]
        h = _modulated_norm(x, gpre_ref[...], mod).astype(BF16)
        act = (jax.nn.silu(_dot(h, wg_ref[...])) * _dot(h, wu_ref[...])).astype(BF16)
        y = _dot(act, wd_ref[...])
        o_ref[0, r:r + chunk] = x + mod[2:3] * _rms(y, gpost_ref[...])


def _ffn(x, mod, gpre, gpost, wg, wu, wd, tm):
    b, s, d = x.shape
    dff = wg.shape[1]
    row = pl.BlockSpec((1, tm, d), lambda i, j: (i, j, 0))
    vec = _resident((1, d), lambda i, j: (0, 0))
    return pl.pallas_call(
        functools.partial(_ffn_kernel, tm=tm, chunk=FFN_ROW_CHUNK),
        grid=(b, s // tm),
        in_specs=[
            row,
            pl.BlockSpec((1, 3, d), lambda i, j: (i, 0, 0)),
            vec, vec,
            _resident((d, dff), lambda i, j: (0, 0)),
            _resident((d, dff), lambda i, j: (0, 0)),
            _resident((dff, d), lambda i, j: (0, 0)),
        ],
        out_specs=row,
        out_shape=jax.ShapeDtypeStruct((b, s, d), F32),
        compiler_params=_params(2),
        name="ffn",
    )(x, mod, gpre, gpost, wg, wu, wd)


def _token_mixer_kernel(x_ref, mod_ref, gpre_ref, gpost_ref, win_ref, lng_ref, ws_ref, bs_ref, wout_ref,
                        o_ref, *, d, tm, groups, chunk):
    mod = mod_ref[0]
    pos_t = lax.broadcasted_iota(jnp.int32, (TM_CHUNK, TM_CHUNK), 0) // CHUNK
    pos_s = lax.broadcasted_iota(jnp.int32, (TM_CHUNK, TM_CHUNK), 1) // CHUNK
    chunk_causal = pos_t >= pos_s
    gd = d // groups
    bias = bs_ref[...]
    w_s = [jnp.where(chunk_causal, ws_ref[g], 0.0).astype(BF16) for g in range(groups)]
    starts = list(range(0, tm, chunk))
    xs = [x_ref[0, r:r + chunk] for r in starts]
    uvs = [_dot(_modulated_norm(x, gpre_ref[...], mod).astype(BF16), win_ref[...]) for x in xs]
    ys = []
    for uv in uvs:
        uv = jax.nn.gelu(uv)
        u = uv[:, :d]
        v = uv[:, d:]
        vc = v - jnp.mean(v, axis=-1, keepdims=True)
        v = (vc * lax.rsqrt(jnp.mean(vc * vc, axis=-1, keepdims=True) + EPS) * lng_ref[...]).astype(BF16)
        rows = []
        for n in range(chunk // TM_CHUNK):
            cols = []
            for g in range(groups):
                v_ng = v[n * TM_CHUNK:(n + 1) * TM_CHUNK, g * gd:(g + 1) * gd]
                cols.append(_dot(w_s[g], v_ng) + bias[:, g:g + 1])
            rows.append(jnp.concatenate(cols, axis=1))
        sv = jnp.concatenate(rows, axis=0) if len(rows) > 1 else rows[0]
        ys.append((u * sv).astype(BF16))
    for r, x, y in zip(starts, xs, ys):
        o_ref[0, r:r + chunk] = x + mod[2:3] * _rms(_dot(y, wout_ref[...]), gpost_ref[...])


def _token_mixer(x, mod, gpre, gpost, w_in, ln_g, w_s, b_s_t, w_out, tm):
    b, s, d = x.shape
    groups, t, _ = w_s.shape
    row = pl.BlockSpec((1, tm, d), lambda i, j: (i, j, 0))
    vec = _resident((1, d), lambda i, j: (0, 0))
    return pl.pallas_call(
        functools.partial(_token_mixer_kernel, d=d, tm=tm, groups=groups, chunk=MIXER_ROW_CHUNK),
        grid=(b, s // tm),
        in_specs=[
            row,
            pl.BlockSpec((1, 3, d), lambda i, j: (i, 0, 0)),
            vec, vec,
            _resident((d, 2 * d), lambda i, j: (0, 0)),
            vec,
            _resident((groups, t, t), lambda i, j: (0, 0, 0)),
            _resident((t, groups), lambda i, j: (0, 0)),
            _resident((d, d), lambda i, j: (0, 0)),
        ],
        out_specs=row,
        out_shape=jax.ShapeDtypeStruct((b, s, d), F32),
        compiler_params=_params(2),
        name="token_mixer",
    )(x, mod, gpre, gpost, w_in, ln_g, w_s, b_s_t, w_out)


def kernel(x, c, mix_ada_w, mix_ada_b, mix_pre_g, mix_post_g, sb_w_qkv, sb_w_o, tm_w_in, tm_ln_g, tm_w_s, tm_b_s, tm_w_out, ffn_ada_w, ffn_ada_b, ffn_pre_g, ffn_post_g, ffn_w_gate, ffn_w_up, ffn_w_down):
    b, s, d = x.shape
    depth = mix_ada_w.shape[0]
    assert depth == 2 and s % ATTN_TILE == 0 and d % LANES == 0

    mix_mod, ffn_mod = _ada(c, mix_ada_w, mix_ada_b, ffn_ada_w, ffn_ada_b)
    mix_mod = mix_mod.reshape(depth, b, 3, d)
    ffn_mod = ffn_mod.reshape(depth, b, 3, d)

    def vec(a, i):
        return a[i].reshape(1, d)

    def ffn_layer(x, i):
        return _ffn(x, ffn_mod[i], vec(ffn_pre_g, i), vec(ffn_post_g, i), ffn_w_gate[i].astype(BF16),
                    ffn_w_up[i].astype(BF16), ffn_w_down[i].astype(BF16), tm=1024)

    qkv = _qkv(x, mix_mod[0], vec(mix_pre_g, 0), sb_w_qkv[0].astype(BF16), tm=1024)
    attn = _sb_attn(qkv, d)
    x = _out_proj(attn, x, mix_mod[0], vec(mix_post_g, 0), sb_w_o[0].astype(BF16), tm=1024)
    x = ffn_layer(x, 0)

    x = _token_mixer(x, mix_mod[1], vec(mix_pre_g, 1), vec(mix_post_g, 1), tm_w_in[0].astype(BF16),
                     vec(tm_ln_g, 0), tm_w_s[0], tm_b_s[0].T, tm_w_out[0].astype(BF16), tm=1024)
    x = ffn_layer(x, 1)
    return x
```

```python
import functools
import math

import jax
import jax.numpy as jnp
from jax import lax
from jax.experimental import pallas as pl
from jax.experimental.pallas import tpu as pltpu

EPS = 1e-6
CHUNK = 64
SB_HEAD_DIM = 64
TM_CHUNK = 128
LANES = 128
ATTN_TILE = 256
FFN_ROW_CHUNK = 256
MIXER_ROW_CHUNK = 256
NEAR_BLOCKS = 2
FAR_SKIP_LOG2 = 160.0
VMEM_LIMIT_BYTES = 56 * 1024 * 1024

F32 = jnp.float32
BF16 = jnp.bfloat16
NEG_BIG = -1e30


def _params(n_axes):
    return pltpu.CompilerParams(
        dimension_semantics=("parallel",) * n_axes,
        vmem_limit_bytes=VMEM_LIMIT_BYTES,
    )


def _resident(block_shape, index_map):
    return pl.BlockSpec(block_shape, index_map, pipeline_mode=pl.Buffered(1))


def _rms(x, g):
    return x * lax.rsqrt(jnp.mean(x * x, axis=-1, keepdims=True) + EPS) * g


def _modulated_norm(x, g, mod):
    return _rms(x, g) * (1.0 + mod[1:2]) + mod[0:1]


def _dot(a, b):
    return jnp.dot(a, b, preferred_element_type=F32)


def _ada_kernel(c_ref, mw_ref, mb_ref, fw_ref, fb_ref, mo_ref, fo_ref):
    s = jax.nn.silu(c_ref[...]).astype(BF16)
    mo_ref[0] = _dot(s, mw_ref[0].astype(BF16)) + mb_ref[0]
    fo_ref[0] = _dot(s, fw_ref[0].astype(BF16)) + fb_ref[0]


def _ada(c, mix_w, mix_b, ffn_w, ffn_b):
    depth, d, d3 = mix_w.shape
    b = c.shape[0]
    nblk = d3 // d
    w_spec = pl.BlockSpec((1, d, d), lambda l, n: (l, 0, n))
    b_spec = pl.BlockSpec((1, 1, d), lambda l, n: (l, 0, n))
    o_spec = pl.BlockSpec((1, b, d), lambda l, n: (l, 0, n))
    out = jax.ShapeDtypeStruct((depth, b, d3), F32)
    return pl.pallas_call(
        _ada_kernel,
        grid=(depth, nblk),
        in_specs=[pl.BlockSpec((b, d), lambda l, n: (0, 0)), w_spec, b_spec, w_spec, b_spec],
        out_specs=[o_spec, o_spec],
        out_shape=[out, out],
        compiler_params=_params(2),
        name="ada",
    )(c, mix_w, mix_b.reshape(depth, 1, d3), ffn_w, ffn_b.reshape(depth, 1, d3))


def _qkv_kernel(x_ref, mod_ref, g_ref, w_ref, o_ref, *, d, q_scale, tm, chunk):
    mod = mod_ref[0]
    for r in range(0, tm, chunk):
        h = _modulated_norm(x_ref[0, r:r + chunk], g_ref[...], mod).astype(BF16)
        acc = _dot(h, w_ref[...])
        o_ref[0, r:r + chunk, :d] = (acc[:, :d] * q_scale).astype(BF16)
        o_ref[0, r:r + chunk, d:] = acc[:, d:].astype(BF16)


def _qkv(x, mod, g, w, tm):
    b, s, d = x.shape
    n = w.shape[1]
    return pl.pallas_call(
        functools.partial(_qkv_kernel, d=d, q_scale=SB_HEAD_DIM ** -0.5 * math.log2(math.e), tm=tm,
                          chunk=FFN_ROW_CHUNK),
        grid=(b, s // tm),
        in_specs=[
            pl.BlockSpec((1, tm, d), lambda i, j: (i, j, 0)),
            pl.BlockSpec((1, 3, d), lambda i, j: (i, 0, 0)),
            _resident((1, d), lambda i, j: (0, 0)),
            _resident((d, n), lambda i, j: (0, 0)),
        ],
        out_specs=pl.BlockSpec((1, tm, n), lambda i, j: (i, j, 0)),
        out_shape=jax.ShapeDtypeStruct((b, s, n), BF16),
        compiler_params=_params(2),
        name="qkv",
    )(x, mod, g, w)


def _sb_tile(qh, k_t, v_t, tri, diag_mask, tile):
    z = lax.dot_general(qh, k_t, (((1,), (1,)), ((), ())), preferred_element_type=F32)
    if diag_mask is not None:
        z = jnp.where(diag_mask, z, NEG_BIG)
    neg_abs = -jnp.abs(z)
    sp = jnp.maximum(z, 0.0) + jnp.log2(1.0 + jnp.exp2(neg_abs))
    later = _dot(sp.astype(BF16), tri)
    a = jnp.exp2((z - sp) - later)
    o = _dot(a.astype(BF16), v_t)
    totals = [jnp.broadcast_to(later[r:r + tile, 0:1] + sp[r:r + tile, 0:1], (tile, LANES))
              for r in range(0, z.shape[0], tile)]
    return o, totals


def _head_lanes(rows, head):
    lane = lax.broadcasted_iota(jnp.int32, (rows, LANES), 1)
    return (lane < SB_HEAD_DIM) if head == 0 else (lane >= SB_HEAD_DIM)


def _sb_attn_kernel(q_ref, k_ref, v_ref, o_ref, acc_ref, carry_ref, *, seq, tile):
    nsub = seq // tile
    row = lax.broadcasted_iota(jnp.int32, (NEAR_BLOCKS * tile, tile), 0)
    col = lax.broadcasted_iota(jnp.int32, (NEAR_BLOCKS * tile, tile), 1)
    causal = (row >= tile) | (col < row)
    trow = lax.broadcasted_iota(jnp.int32, (tile, tile), 0)
    tcol = lax.broadcasted_iota(jnp.int32, (tile, tile), 1)
    tri = (trow > tcol).astype(BF16)
    causal_last = tcol < trow

    def q_rows(head, lo, hi):
        return jnp.where(_head_lanes(hi - lo, head), q_ref[0, lo:hi, :], 0)

    acc = [[None] * nsub for _ in range(2)]
    carry = [[None] * nsub for _ in range(2)]
    for j in range(nsub - 1, -1, -1):
        nb = min(NEAR_BLOCKS, nsub - j)
        lo, hi = j * tile, (j + nb) * tile
        k_t = k_ref[0, lo:lo + tile, :]
        v_t = v_ref[0, lo:lo + tile, :]
        mask = causal if nb == NEAR_BLOCKS else causal_last
        q_both = jnp.concatenate([q_rows(0, lo, hi), q_rows(1, lo, hi)], axis=0)
        o, totals = _sb_tile(q_both, k_t, v_t, tri, jnp.concatenate([mask, mask], axis=0), tile)
        for head in range(2):
            base = head * nb
            acc[head][j], carry[head][j] = o[base * tile:(base + 1) * tile], totals[base]
            for b in range(1, nb):
                r = j + b
                acc[head][r] = acc[head][r] + o[(base + b) * tile:(base + b + 1) * tile] * jnp.exp2(-carry[head][r])
                carry[head][r] = carry[head][r] + totals[base + b]
    block_low = []
    for head in range(2):
        for r in range(nsub):
            acc_ref[head, r * tile:(r + 1) * tile, :] = acc[head][r]
            carry_ref[head, r * tile:(r + 1) * tile, :] = carry[head][r]
        block_low.append([jnp.min(c) for c in carry[head]])

    for head in range(2):
        for r in range(NEAR_BLOCKS, nsub):
            rows = slice(r * tile, (r + 1) * tile)

            def keep_going(state):
                j, low = state
                return (j >= 0) & (low < FAR_SKIP_LOG2)

            def one_tile(state, head=head, r=r, rows=rows):
                j, _ = state
                keys = pl.ds(pl.multiple_of(j * tile, tile), tile)
                o, totals = _sb_tile(q_rows(head, r * tile, (r + 1) * tile), k_ref[0, keys, :], v_ref[0, keys, :],
                                     tri, None, tile)
                c = carry_ref[head, rows, :]
                acc_ref[head, rows, :] = acc_ref[head, rows, :] + o * jnp.exp2(-c)
                c = c + totals[0]
                carry_ref[head, rows, :] = c
                return j - 1, jnp.min(c)

            lax.while_loop(keep_going, one_tile, (jnp.int32(r - NEAR_BLOCKS), block_low[head][r]))

    o_ref[0] = jnp.where(_head_lanes(seq, 0), acc_ref[0], acc_ref[1]).astype(BF16)


def _sb_attn(qkv, d):
    b, s, _ = qkv.shape
    pairs = d // LANES

    def spec(offset):
        return pl.BlockSpec((1, s, LANES), lambda i, p: (i, 0, p + offset))

    return pl.pallas_call(
        functools.partial(_sb_attn_kernel, seq=s, tile=ATTN_TILE),
        grid=(b, pairs),
        in_specs=[spec(0), spec(pairs), spec(2 * pairs)],
        out_specs=pl.BlockSpec((1, s, LANES), lambda i, p: (i, 0, p)),
        out_shape=jax.ShapeDtypeStruct((b, s, d), BF16),
        scratch_shapes=[pltpu.VMEM((2, s, LANES), F32), pltpu.VMEM((2, s, LANES), F32)],
        compiler_params=_params(2),
        name="sb_attn",
    )(qkv, qkv, qkv)


def _out_proj_kernel(a_ref, x_ref, mod_ref, g_ref, w_ref, o_ref):
    y = _dot(a_ref[0], w_ref[...])
    o_ref[0] = x_ref[0] + mod_ref[0][2:3] * _rms(y, g_ref[...])


def _out_proj(a, x, mod, g, w, tm):
    b, s, d = x.shape
    row = pl.BlockSpec((1, tm, d), lambda i, j: (i, j, 0))
    return pl.pallas_call(
        _out_proj_kernel,
        grid=(b, s // tm),
        in_specs=[
            row, row,
            pl.BlockSpec((1, 3, d), lambda i, j: (i, 0, 0)),
            _resident((1, d), lambda i, j: (0, 0)),
            _resident((d, d), lambda i, j: (0, 0)),
        ],
        out_specs=row,
        out_shape=jax.ShapeDtypeStruct((b, s, d), F32),
        compiler_params=_params(2),
        name="out_proj",
    )(a, x, mod, g, w)


def _ffn_kernel(x_ref, mod_ref, gpre_ref, gpost_ref, wg_ref, wu_ref, wd_ref, o_ref, *, tm, chunk):
    mod = mod_ref[0]
    for r in range(0, tm, chunk):
        x = x_ref[0, r:r + chunk]
        h = _modulated_norm(x, gpre_ref[...], mod).astype(BF16)
        act = (jax.nn.silu(_dot(h, wg_ref[...])) * _dot(h, wu_ref[...])).astype(BF16)
        y = _dot(act, wd_ref[...])
        o_ref[0, r:r + chunk] = x + mod[2:3] * _rms(y, gpost_ref[...])


def _ffn(x, mod, gpre, gpost, wg, wu, wd, tm):
    b, s, d = x.shape
    dff = wg.shape[1]
    row = pl.BlockSpec((1, tm, d), lambda i, j: (i, j, 0))
    vec = _resident((1, d), lambda i, j: (0, 0))
    return pl.pallas_call(
        functools.partial(_ffn_kernel, tm=tm, chunk=FFN_ROW_CHUNK),
        grid=(b, s // tm),
        in_specs=[
            row,
            pl.BlockSpec((1, 3, d), lambda i, j: (i, 0, 0)),
            vec, vec,
            _resident((d, dff), lambda i, j: (0, 0)),
            _resident((d, dff), lambda i, j: (0, 0)),
            _resident((dff, d), lambda i, j: (0, 0)),
        ],
        out_specs=row,
        out_shape=jax.ShapeDtypeStruct((b, s, d), F32),
        compiler_params=_params(2),
        name="ffn",
    )(x, mod, gpre, gpost, wg, wu, wd)


def _token_mixer_kernel(x_ref, mod_ref, gpre_ref, gpost_ref, win_ref, lng_ref, ws_ref, bs_ref, wout_ref,
                        o_ref, *, d, tm, groups, chunk):
    mod = mod_ref[0]
    pos_t = lax.broadcasted_iota(jnp.int32, (TM_CHUNK, TM_CHUNK), 0) // CHUNK
    pos_s = lax.broadcasted_iota(jnp.int32, (TM_CHUNK, TM_CHUNK), 1) // CHUNK
    chunk_causal = pos_t >= pos_s
    gd = d // groups
    bias = bs_ref[...]
    w_s = [jnp.where(chunk_causal, ws_ref[g], 0.0).astype(BF16) for g in range(groups)]
    starts = list(range(0, tm, chunk))
    xs = [x_ref[0, r:r + chunk] for r in starts]
    uvs = [_dot(_modulated_norm(x, gpre_ref[...], mod).astype(BF16), win_ref[...]) for x in xs]
    ys = []
    for uv in uvs:
        uv = jax.nn.gelu(uv)
        u = uv[:, :d]
        v = uv[:, d:]
        vc = v - jnp.mean(v, axis=-1, keepdims=True)
        v = (vc * lax.rsqrt(jnp.mean(vc * vc, axis=-1, keepdims=True) + EPS) * lng_ref[...]).astype(BF16)
        rows = []
        for n in range(chunk // TM_CHUNK):
            cols = []
            for g in range(groups):
                v_ng = v[n * TM_CHUNK:(n + 1) * TM_CHUNK, g * gd:(g + 1) * gd]
                cols.append(_dot(w_s[g], v_ng) + bias[:, g:g + 1])
            rows.append(jnp.concatenate(cols, axis=1))
        sv = jnp.concatenate(rows, axis=0) if len(rows) > 1 else rows[0]
        ys.append((u * sv).astype(BF16))
    for r, x, y in zip(starts, xs, ys):
        o_ref[0, r:r + chunk] = x + mod[2:3] * _rms(_dot(y, wout_ref[...]), gpost_ref[...])


def _token_mixer(x, mod, gpre, gpost, w_in, ln_g, w_s, b_s_t, w_out, tm):
    b, s, d = x.shape
    groups, t, _ = w_s.shape
    row = pl.BlockSpec((1, tm, d), lambda i, j: (i, j, 0))
    vec = _resident((1, d), lambda i, j: (0, 0))
    return pl.pallas_call(
        functools.partial(_token_mixer_kernel, d=d, tm=tm, groups=groups, chunk=MIXER_ROW_CHUNK),
        grid=(b, s // tm),
        in_specs=[
            row,
            pl.BlockSpec((1, 3, d), lambda i, j: (i, 0, 0)),
            vec, vec,
            _resident((d, 2 * d), lambda i, j: (0, 0)),
            vec,
            _resident((groups, t, t), lambda i, j: (0, 0, 0)),
            _resident((t, groups), lambda i, j: (0, 0)),
            _resident((d, d), lambda i, j: (0, 0)),
        ],
        out_specs=row,
        out_shape=jax.ShapeDtypeStruct((b, s, d), F32),
        compiler_params=_params(2),
        name="token_mixer",
    )(x, mod, gpre, gpost, w_in, ln_g, w_s, b_s_t, w_out)


def kernel(x, c, mix_ada_w, mix_ada_b, mix_pre_g, mix_post_g, sb_w_qkv, sb_w_o, tm_w_in, tm_ln_g, tm_w_s, tm_b_s, tm_w_out, ffn_ada_w, ffn_ada_b, ffn_pre_g, ffn_post_g, ffn_w_gate, ffn_w_up, ffn_w_down):
    b, s, d = x.shape
    depth = mix_ada_w.shape[0]
    assert depth == 2 and s % ATTN_TILE == 0 and d % LANES == 0

    mix_mod, ffn_mod = _ada(c, mix_ada_w, mix_ada_b, ffn_ada_w, ffn_ada_b)
    mix_mod = mix_mod.reshape(depth, b, 3, d)
    ffn_mod = ffn_mod.reshape(depth, b, 3, d)

    def vec(a, i):
        return a[i].reshape(1, d)

    def ffn_layer(x, i):
        return _ffn(x, ffn_mod[i], vec(ffn_pre_g, i), vec(ffn_post_g, i), ffn_w_gate[i].astype(BF16),
                    ffn_w_up[i].astype(BF16), ffn_w_down[i].astype(BF16), tm=1024)

    qkv = _qkv(x, mix_mod[0], vec(mix_pre_g, 0), sb_w_qkv[0].astype(BF16), tm=1024)
    attn = _sb_attn(qkv, d)
    x = _out_proj(attn, x, mix_mod[0], vec(mix_post_g, 0), sb_w_o[0].astype(BF16), tm=1024)
    x = ffn_layer(x, 0)

    x = _token_mixer(x, mix_mod[1], vec(mix_pre_g, 1), vec(mix_post_g, 1), tm_w_in[0].astype(BF16),
                     vec(tm_ln_g, 0), tm_w_s[0], tm_b_s[0].T, tm_w_out[0].astype(BF16), tm=1024)
    x = ffn_layer(x, 1)
    return x
```
